```python
import math
import jax, jax.numpy as jnp
from jax import lax
import numpy as np

D_MODEL = 1024
BATCH = 8
SEQ = 2048
DEPTH = 4

CHUNK = 64
MEM_LEN = 256
Q_BLOCK = 128

FOX_HEADS = 8
FOX_HEAD_DIM = 64
FOX_WIDTH = FOX_HEADS * FOX_HEAD_DIM
FORGET_BIAS = 2.0
SSM_GROUP = 16
SSM_WIDTH = 512
SSM_GROUPS = SSM_WIDTH // SSM_GROUP
SSM_STATE = 64
DT_MIN = 1e-3
DT_MAX = 1e-1
CA_HEADS = 8
CA_HEAD_DIM = 64
CA_WIDTH = CA_HEADS * CA_HEAD_DIM
CA_LEFT_CHUNKS = 8
CA_BAND = (CA_LEFT_CHUNKS + 1) * CHUNK
REL_MIN = -(CHUNK - 1)
REL_MAX = 4 * CHUNK
N_REL = REL_MAX - REL_MIN + 1
N_BRANCH = 3
XA_HEADS = 4
XA_HEAD_DIM = D_MODEL // XA_HEADS
D_FF = 2816
CONV_WIDTH = 3
DN_ALPHA = (2 * DEPTH) ** 0.25
DN_BETA = (8 * DEPTH) ** -0.25
LN_EPS = 1e-5
NEG_INF = -1e30

_SPLITS = (FOX_WIDTH, FOX_WIDTH, FOX_WIDTH, FOX_HEADS, SSM_WIDTH,
           CA_WIDTH, CA_WIDTH, CA_WIDTH, D_MODEL, D_MODEL, D_MODEL)
N_IN = sum(_SPLITS)

kernel_name = "hybrid_fox_s5_chunkattn_deepnorm"


def _layer_norm(x, g, b):
    xf = x.astype(jnp.float32)
    mu = jnp.mean(xf, axis=-1, keepdims=True)
    var = jnp.mean(jnp.square(xf - mu), axis=-1, keepdims=True)
    y = (xf - mu) * lax.rsqrt(var + LN_EPS) * g.astype(jnp.float32) + b.astype(jnp.float32)
    return y.astype(x.dtype)


def _forgetting_attention(q, k, v, f_logit):
    S = q.shape[1]
    cum = jnp.cumsum(jax.nn.log_sigmoid(f_logit.astype(jnp.float32)), axis=1)
    cum = jnp.transpose(cum, (0, 2, 1))
    scale = FOX_HEAD_DIM ** -0.5
    outs = []
    for blk in range(S // Q_BLOCK):
        q0, q1 = blk * Q_BLOCK, (blk + 1) * Q_BLOCK
        s = jnp.einsum('bqhd,bkhd->bhqk', q[:, q0:q1], k[:, :q1]).astype(jnp.float32) * scale
        s = s + cum[:, :, q0:q1, None] - cum[:, :, None, :q1]
        causal = jnp.arange(q0, q1)[:, None] >= jnp.arange(q1)[None, :]
        p = jax.nn.softmax(jnp.where(causal, s, NEG_INF), axis=-1)
        outs.append(jnp.einsum('bhqk,bkhd->bqhd', p.astype(v.dtype), v[:, :q1]))
    return jnp.concatenate(outs, axis=1)


def _complex_linear_combine(e1, e2):
    a1r, a1i, b1r, b1i = e1
    a2r, a2i, b2r, b2i = e2
    ar = a2r * a1r - a2i * a1i
    ai = a2r * a1i + a2i * a1r
    br = a2r * b1r - a2i * b1i + b2r
    bi = a2r * b1i + a2i * b1r + b2i
    return ar, ai, br, bi


def _s5(u, lam_re, lam_im, log_dt, b_re, b_im, c_re, c_im, d):
    f32 = jnp.float32
    Bsz, S, _ = u.shape
    uf = u.astype(f32)
    ug = uf.reshape(Bsz, S, SSM_GROUPS, SSM_GROUP)
    lr = jnp.minimum(lam_re.astype(f32), -1e-4)
    li = lam_im.astype(f32)
    dt = jnp.exp(log_dt.astype(f32))[:, None]
    mag = jnp.exp(lr * dt)
    ar = mag * jnp.cos(li * dt)
    ai = mag * jnp.sin(li * dt)
    den = lr * lr + li * li
    gr = ((ar - 1.0) * lr + ai * li) / den
    gi = (ai * lr - (ar - 1.0) * li) / den
    br, bi = b_re.astype(f32), b_im.astype(f32)
    bbr = gr[..., None] * br - gi[..., None] * bi
    bbi = gr[..., None] * bi + gi[..., None] * br
    xr = jnp.einsum('bsgc,gpc->bsgp', ug, bbr)
    xi = jnp.einsum('bsgc,gpc->bsgp', ug, bbi)
    a_r = jnp.broadcast_to(ar, xr.shape)
    a_i = jnp.broadcast_to(ai, xr.shape)
    _, _, hr, hi = lax.associative_scan(_complex_linear_combine, (a_r, a_i, xr, xi), axis=1)
    y = (jnp.einsum('bsgp,gcp->bsgc', hr, c_re.astype(f32))
         - jnp.einsum('bsgp,gcp->bsgc', hi, c_im.astype(f32)))
    y = y.reshape(Bsz, S, SSM_WIDTH) + d.astype(f32) * uf
    return y.astype(u.dtype)


def _chunk_attention(q, k, v, rel_bias):
    Bsz, S, H, Dh = q.shape
    NC = S // CHUNK
    qc = q.reshape(Bsz, NC, CHUNK, H, Dh)

    def band(t):
        tc = t.reshape(Bsz, NC, CHUNK, H, Dh)
        tp = jnp.pad(tc, ((0, 0), (CA_LEFT_CHUNKS, 0), (0, 0), (0, 0), (0, 0)))
        return jnp.concatenate([tp[:, j:j + NC] for j in range(CA_LEFT_CHUNKS + 1)], axis=2)

    kb, vb = band(k), band(v)
    s = jnp.einsum('bnqhd,bnkhd->bnhqk', qc, kb).astype(jnp.float32) * (CA_HEAD_DIM ** -0.5)
    qi = jnp.arange(CHUNK)
    kk = jnp.arange(CA_BAND)
    rel = CA_LEFT_CHUNKS * CHUNK + qi[:, None] - kk[None, :]
    idx = jnp.clip(rel, REL_MIN, REL_MAX) - REL_MIN
    bias = rel_bias.astype(jnp.float32)[:, idx]
    valid = (jnp.arange(NC)[:, None] + (kk // CHUNK)[None, :] - CA_LEFT_CHUNKS) >= 0
    s = jnp.where(valid[None, :, None, None, :], s + bias[None, None], NEG_INF)
    p = jax.nn.softmax(s, axis=-1)
    o = jnp.einsum('bnhqk,bnkhd->bnqhd', p.astype(v.dtype), vb)
    return o.reshape(Bsz, S, H * Dh)


def _hybrid_mixer(x, w_in, b_in, lam_re, lam_im, log_dt, b_re, b_im, c_re, c_im, ssm_d,
                  rel_bias, w_fox_o, w_ssm_glu, w_ca_o, w_o):
    Bsz, S, _ = x.shape
    z = x @ w_in + b_in
    fq, fk, fv, ff, su, cq, ck, cv, g_fox, g_ssm, g_ca = jnp.split(z, np.cumsum(_SPLITS)[:-1], axis=-1)
    hs = lambda t, h, dh: t.reshape(Bsz, S, h, dh)
    ya = _forgetting_attention(hs(fq, FOX_HEADS, FOX_HEAD_DIM), hs(fk, FOX_HEADS, FOX_HEAD_DIM),
                               hs(fv, FOX_HEADS, FOX_HEAD_DIM), ff)
    ya = ya.reshape(Bsz, S, FOX_WIDTH) @ w_fox_o
    yb = jax.nn.gelu(_s5(su, lam_re, lam_im, log_dt, b_re, b_im, c_re, c_im, ssm_d)) @ w_ssm_glu
    yb_val, yb_gate = jnp.split(yb, 2, axis=-1)
    yb = yb_val * jax.nn.sigmoid(yb_gate)
    yc = _chunk_attention(hs(cq, CA_HEADS, CA_HEAD_DIM), hs(ck, CA_HEADS, CA_HEAD_DIM),
                          hs(cv, CA_HEADS, CA_HEAD_DIM), rel_bias) @ w_ca_o
    merged = (jax.nn.sigmoid(g_fox) * ya + jax.nn.sigmoid(g_ssm) * yb + jax.nn.sigmoid(g_ca) * yc)
    return merged @ w_o


def _memory_cross_attention(x, mem, wq, wkv, wo):
    Bsz, S, _ = x.shape
    M = mem.shape[1]
    q = (x @ wq).reshape(Bsz, S, XA_HEADS, XA_HEAD_DIM)
    k, v = jnp.split(mem @ wkv, 2, axis=-1)
    k = k.reshape(Bsz, M, XA_HEADS, XA_HEAD_DIM)
    v = v.reshape(Bsz, M, XA_HEADS, XA_HEAD_DIM)
    s = jnp.einsum('bqhd,bkhd->bhqk', q, k).astype(jnp.float32) * (XA_HEAD_DIM ** -0.5)
    p = jax.nn.softmax(s, axis=-1)
    o = jnp.einsum('bhqk,bkhd->bqhd', p.astype(v.dtype), v)
    return o.reshape(Bsz, S, D_MODEL) @ wo


def _conv_ffn(x, w_up, conv_w, conv_b, w_down):
    a, g = jnp.split(x @ w_up, 2, axis=-1)
    a = lax.conv_general_dilated(a, conv_w[:, None, :], window_strides=(1,),
                                 padding=[(CONV_WIDTH - 1, 0)],
                                 dimension_numbers=('NWC', 'WIO', 'NWC'),
                                 feature_group_count=D_FF) + conv_b
    return (jax.nn.gelu(a) * g) @ w_down


def setup_inputs(seed: int = 0) -> dict:
    key = jax.random.key(seed)
    ks = jax.random.split(key, 26)
    f32 = jnp.float32
    L, D = DEPTH, D_MODEL
    nrm = lambda k, shape, s: jax.random.normal(k, shape, f32) * s
    sd = D ** -0.5
    col_scale = jnp.concatenate([
        jnp.full((FOX_WIDTH,), sd, f32), jnp.full((FOX_WIDTH,), sd, f32),
        jnp.full((FOX_WIDTH,), sd * DN_BETA, f32), jnp.full((FOX_HEADS,), sd, f32),
        jnp.full((SSM_WIDTH,), sd, f32),
        jnp.full((CA_WIDTH,), sd, f32), jnp.full((CA_WIDTH,), sd, f32),
        jnp.full((CA_WIDTH,), sd * DN_BETA, f32),
        jnp.full((N_BRANCH * D,), sd, f32)])
    col_offset = jnp.concatenate([
        jnp.zeros((3 * FOX_WIDTH,), f32), jnp.full((FOX_HEADS,), FORGET_BIAS, f32),
        jnp.zeros((N_IN - 3 * FOX_WIDTH - FOX_HEADS,), f32)])
    xa_kv_scale = jnp.concatenate([jnp.full((D,), sd, f32), jnp.full((D,), sd * DN_BETA, f32)])
    lam_im0 = math.pi * jnp.arange(SSM_STATE, dtype=f32)
    return {
        "x": nrm(ks[0], (BATCH, SEQ, D), 1.0),
        "mem": nrm(ks[1], (BATCH, MEM_LEN, D), 1.0),
        "w_in": jax.random.normal(ks[2], (L, D, N_IN), f32) * col_scale,
        "b_in": nrm(ks[3], (L, N_IN), 0.02) + col_offset,
        "ssm_lambda_re": -0.5 + nrm(ks[4], (L, SSM_GROUPS, SSM_STATE), 1e-3),
        "ssm_lambda_im": lam_im0 + nrm(ks[5], (L, SSM_GROUPS, SSM_STATE), 1e-3),
        "ssm_log_dt": jax.random.uniform(ks[6], (L, SSM_GROUPS), f32, math.log(DT_MIN), math.log(DT_MAX)),
        "ssm_b_re": nrm(ks[7], (L, SSM_GROUPS, SSM_STATE, SSM_GROUP), (2 * SSM_GROUP) ** -0.5),
        "ssm_b_im": nrm(ks[8], (L, SSM_GROUPS, SSM_STATE, SSM_GROUP), (2 * SSM_GROUP) ** -0.5),
        "ssm_c_re": nrm(ks[9], (L, SSM_GROUPS, SSM_GROUP, SSM_STATE), SSM_STATE ** -0.5),
        "ssm_c_im": nrm(ks[10], (L, SSM_GROUPS, SSM_GROUP, SSM_STATE), SSM_STATE ** -0.5),
        "ssm_d": nrm(ks[11], (L, SSM_WIDTH), 1.0),
        "ca_rel_bias": nrm(ks[12], (L, CA_HEADS, N_REL), 0.1),
        "w_fox_o": nrm(ks[13], (L, FOX_WIDTH, D), DN_BETA * FOX_WIDTH ** -0.5),
        "w_ssm_glu": nrm(ks[14], (L, SSM_WIDTH, 2 * D), DN_BETA * SSM_WIDTH ** -0.5),
        "w_ca_o": nrm(ks[15], (L, CA_WIDTH, D), DN_BETA * CA_WIDTH ** -0.5),
        "w_o": nrm(ks[16], (L, D, D), DN_BETA * sd),
        "xa_wq": nrm(ks[17], (L, D, D), sd),
        "xa_wkv": jax.random.normal(ks[18], (L, D, 2 * D), f32) * xa_kv_scale,
        "xa_wo": nrm(ks[19], (L, D, D), DN_BETA * sd),
        "ffn_w_up": nrm(ks[20], (L, D, 2 * D_FF), DN_BETA * sd),
        "ffn_conv_w": nrm(ks[21], (L, CONV_WIDTH, D_FF), CONV_WIDTH ** -0.5),
        "ffn_conv_b": nrm(ks[22], (L, D_FF), 0.02),
        "ffn_w_down": nrm(ks[23], (L, D_FF, D), DN_BETA * D_FF ** -0.5),
        "ln_g": 1.0 + nrm(ks[24], (L, 3, D), 0.02),
        "ln_b": nrm(ks[25], (L, 3, D), 0.02),
    }


def reference(x, mem, w_in, b_in, ssm_lambda_re, ssm_lambda_im, ssm_log_dt, ssm_b_re, ssm_b_im,
              ssm_c_re, ssm_c_im, ssm_d, ca_rel_bias, w_fox_o, w_ssm_glu, w_ca_o, w_o,
              xa_wq, xa_wkv, xa_wo, ffn_w_up, ffn_conv_w, ffn_conv_b, ffn_w_down, ln_g, ln_b):
    for l in range(DEPTH):
        h = _hybrid_mixer(x, w_in[l], b_in[l], ssm_lambda_re[l], ssm_lambda_im[l], ssm_log_dt[l],
                          ssm_b_re[l], ssm_b_im[l], ssm_c_re[l], ssm_c_im[l], ssm_d[l],
                          ca_rel_bias[l], w_fox_o[l], w_ssm_glu[l], w_ca_o[l], w_o[l])
        x = _layer_norm(DN_ALPHA * x + h, ln_g[l, 0], ln_b[l, 0])
        h = _memory_cross_attention(x, mem, xa_wq[l], xa_wkv[l], xa_wo[l])
        x = _layer_norm(DN_ALPHA * x + h, ln_g[l, 1], ln_b[l, 1])
        h = _conv_ffn(x, ffn_w_up[l], ffn_conv_w[l], ffn_conv_b[l], ffn_w_down[l])
        x = _layer_norm(DN_ALPHA * x + h, ln_g[l, 2], ln_b[l, 2])
    return x
```

```python
import functools
import math

import jax
import jax.numpy as jnp
import numpy as np
from jax import lax
from jax.experimental import pallas as pl
from jax.experimental.pallas import tpu as pltpu

F32 = jnp.float32
BF16 = jnp.bfloat16

CHUNK = 64
Q_BLOCK = 128
FOX_HEADS = 8
FOX_HEAD_DIM = 64
FOX_WIDTH = FOX_HEADS * FOX_HEAD_DIM
SSM_GROUP = 16
SSM_WIDTH = 512
SSM_GROUPS = SSM_WIDTH // SSM_GROUP
SSM_STATE = 64
CA_HEADS = 8
CA_HEAD_DIM = 64
CA_WIDTH = CA_HEADS * CA_HEAD_DIM
CA_LEFT_CHUNKS = 8
REL_MIN = -(CHUNK - 1)
REL_MAX = 4 * CHUNK
XA_HEADS = 4
CONV_WIDTH = 3
LN_EPS = 1e-5
NEG_INF = -1e30

LANES = 128
HEAD_PAIR = LANES // FOX_HEAD_DIM
ATT_BLOCK = 256
CA_WINDOW_BLOCKS = 3
CA_TABLE_BLOCKS = 5
SSM_SLAB_GROUPS = LANES // SSM_GROUP
SSM_SLABS = SSM_GROUPS // SSM_SLAB_GROUPS
SSM_SLAB_STATE = SSM_SLAB_GROUPS * SSM_STATE
SSM_TIME_BLOCK = 128
VMEM_LIMIT = 56 * 1024 * 1024

_ARB = "arbitrary"


def _params(n_axes):
    return pltpu.CompilerParams(dimension_semantics=(_ARB,) * n_axes,
                                vmem_limit_bytes=VMEM_LIMIT)


def _layer_norm(v, g, b):
    mu = jnp.mean(v, axis=-1, keepdims=True)
    d = v - mu
    var = jnp.mean(d * d, axis=-1, keepdims=True)
    return d * lax.rsqrt(var + LN_EPS) * g + b


def _dot(a, b):
    return jnp.dot(a, b, preferred_element_type=F32)


def _dot_nt(a, b):
    return lax.dot_general(a, b, (((1,), (1,)), ((), ())), preferred_element_type=F32)


def _mm_kernel(x_ref, w_ref, b_ref, o_ref):
    o_ref[...] = (_dot(x_ref[...], w_ref[...]) + b_ref[...]).astype(o_ref.dtype)


def _matmul(x, w, bias, out_dtype, *, tm, tn, time_major=False, name="matmul"):
    B, S, K = x.shape
    N = w.shape[1]
    tm, tn = min(tm, S), min(tn, N)
    grid = (B, S // tm, N // tn)
    if time_major:
        out_shape = jax.ShapeDtypeStruct((S, B * N), out_dtype)
        out_spec = pl.BlockSpec((tm, tn), lambda b, i, j: (i, b * (N // tn) + j))
    else:
        out_shape = jax.ShapeDtypeStruct((B, S, N), out_dtype)
        out_spec = pl.BlockSpec((None, tm, tn), lambda b, i, j: (b, i, j))
    return pl.pallas_call(
        _mm_kernel,
        out_shape=out_shape,
        grid=grid,
        in_specs=[pl.BlockSpec((None, tm, K), lambda b, i, j: (b, i, 0)),
                  pl.BlockSpec((K, tn), lambda b, i, j: (0, j)),
                  pl.BlockSpec((1, tn), lambda b, i, j: (0, j))],
        out_specs=out_spec,
        compiler_params=_params(3),
        name=name,
    )(x, w, bias)


def _split3(c):
    hi = c.astype(BF16).astype(F32)
    r = c - hi
    mid = r.astype(BF16).astype(F32)
    lo = (r - mid).astype(BF16).astype(F32)
    return hi, mid, lo


def _cum_kernel(ff_ref, o_ref, *, blk):
    S = ff_ref.shape[0]
    row = lax.broadcasted_iota(jnp.int32, (blk, blk), 0)
    col = lax.broadcasted_iota(jnp.int32, (blk, blk), 1)
    tri = jnp.where(row >= col, 1.0, 0.0).astype(BF16)
    carry = jnp.zeros((1, LANES), F32)
    for i in range(S // blk):
        x = ff_ref[i * blk:(i + 1) * blk, :]
        ls = jnp.minimum(x, 0.0) - jnp.log(1.0 + jnp.exp(-jnp.abs(x)))
        hi, mid, lo = _split3(ls)
        cs = (_dot(tri, hi.astype(BF16)) + _dot(tri, mid.astype(BF16))
              + _dot(tri, lo.astype(BF16))) + carry
        o_ref[i * blk:(i + 1) * blk, :] = cs
        carry = cs[blk - 1:blk, :]


def _forget_cumsum(ff):
    B, S, L = ff.shape
    return pl.pallas_call(
        functools.partial(_cum_kernel, blk=min(ATT_BLOCK, S)),
        out_shape=jax.ShapeDtypeStruct((B, S, L), F32),
        grid=(B,),
        in_specs=[pl.BlockSpec((None, S, L), lambda b: (b, 0, 0))],
        out_specs=pl.BlockSpec((None, S, L), lambda b: (b, 0, 0)),
        compiler_params=_params(1),
        name="forget_cumsum",
    )(ff)


def _head_lane_mask(j, shape):
    lane = lax.broadcasted_iota(jnp.int32, shape, len(shape) - 1)
    lo = j * FOX_HEAD_DIM
    return (lane >= lo) & (lane < lo + FOX_HEAD_DIM)


def _decay_lanes(cum_col, j, rows, query_side):
    a0 = (1 - j) * FOX_HEAD_DIM
    lane = lax.broadcasted_iota(jnp.int32, (rows, LANES), 1) - a0
    hi, mid, lo = _split3(cum_col)
    if query_side:
        vals = (hi, mid, lo, 1.0, 1.0, 1.0)
    else:
        vals = (1.0, 1.0, 1.0, -hi, -mid, -lo)
    out = jnp.zeros((rows, LANES), F32)
    for i, v in enumerate(vals):
        out = jnp.where(lane == i, v, out)
    return out


def _cum_column(cum_blk, h):
    lane = lax.broadcasted_iota(jnp.int32, cum_blk.shape, 1)
    return jnp.sum(jnp.where(lane == h, cum_blk, 0.0), axis=1, keepdims=True)


def _fox_kernel(q_ref, k_ref, v_ref, cum_ref, o_ref, kaug_ref, *, blk):
    p = pl.program_id(1)
    qi = pl.program_id(2)
    S = k_ref.shape[0]

    @pl.when(qi == 0)
    def _build_keys():
        for r in range(S // blk):
            rows = slice(r * blk, (r + 1) * blk)
            kf = k_ref[rows, :].astype(F32)
            cum_blk = cum_ref[rows, :]
            for j in range(HEAD_PAIR):
                col = _cum_column(cum_blk, HEAD_PAIR * p + j)
                aug = _decay_lanes(col, j, blk, query_side=False)
                kaug_ref[j, rows, :] = jnp.where(_head_lane_mask(j, kf.shape), kf, aug).astype(BF16)

    q0 = pl.multiple_of(qi * blk, blk)
    qf = q_ref[...].astype(F32) * (FOX_HEAD_DIM ** -0.5)
    cum_q = cum_ref[pl.ds(q0, blk), :]
    qaug = []
    for j in range(HEAD_PAIR):
        col = _cum_column(cum_q, HEAD_PAIR * p + j)
        aug = _decay_lanes(col, j, blk, query_side=True)
        qaug.append(jnp.where(_head_lane_mask(j, qf.shape), qf, aug).astype(BF16))

    row = lax.broadcasted_iota(jnp.int32, (blk, blk), 0)
    colk = lax.broadcasted_iota(jnp.int32, (blk, blk), 1)
    causal = row >= colk

    def step(kv, carry, masked):
        k0 = pl.multiple_of(kv * blk, blk)
        vb = v_ref[pl.ds(k0, blk), :]
        out = []
        for j in range(HEAD_PAIR):
            m, l, acc = carry[j]
            s = _dot_nt(qaug[j], kaug_ref[j, pl.ds(k0, blk), :])
            if masked:
                s = jnp.where(causal, s, NEG_INF)
            m_new = jnp.maximum(m, jnp.max(s, axis=-1, keepdims=True))
            alpha = jnp.exp(m - m_new)
            pr = jnp.exp(s - m_new)
            l = alpha * l + jnp.sum(pr, axis=-1, keepdims=True)
            acc = alpha * acc + _dot(pr.astype(BF16), vb)
            out.append((m_new, l, acc))
        return tuple(out)

    init = tuple((jnp.full((blk, 1), NEG_INF, F32), jnp.zeros((blk, 1), F32),
                  jnp.zeros((blk, LANES), F32)) for _ in range(HEAD_PAIR))
    carry = lax.fori_loop(0, qi, lambda kv, c: step(kv, c, False), init)
    carry = step(qi, carry, True)
    outs = [acc / l for (_, l, acc) in carry]
    o_ref[...] = jnp.where(_head_lane_mask(0, outs[0].shape), outs[0], outs[1]).astype(o_ref.dtype)


def _fox_attention(z, cum, base):
    B, S, _ = z.shape
    blk = min(ATT_BLOCK, S)
    n_pair = FOX_WIDTH // LANES
    return pl.pallas_call(
        functools.partial(_fox_kernel, blk=blk),
        out_shape=jax.ShapeDtypeStruct((B, S, FOX_WIDTH), BF16),
        grid=(B, n_pair, S // blk),
        in_specs=[pl.BlockSpec((None, blk, LANES), lambda b, p, i: (b, i, base + p)),
                  pl.BlockSpec((None, S, LANES), lambda b, p, i: (b, 0, base + n_pair + p)),
                  pl.BlockSpec((None, S, LANES), lambda b, p, i: (b, 0, base + 2 * n_pair + p)),
                  pl.BlockSpec((None, S, LANES), lambda b, p, i: (b, 0, 0))],
        out_specs=pl.BlockSpec((None, blk, LANES), lambda b, p, i: (b, i, p)),
        scratch_shapes=[pltpu.VMEM((HEAD_PAIR, S, LANES), BF16)],
        compiler_params=_params(3),
        name="fox_attention",
    )(z, z, z, cum)


def _ca_kernel(q_ref, k_ref, v_ref, u_ref, o_ref, *, blk):
    j = pl.program_id(2)
    left = CA_WINDOW_BLOCKS - 1
    first = jnp.maximum(j - left, 0)
    w = left - jnp.minimum(j, left)
    qf = q_ref[...].astype(F32) * (CA_HEAD_DIM ** -0.5)
    outs = []
    for hh in range(HEAD_PAIR):
        qm = jnp.where(_head_lane_mask(hh, qf.shape), qf, 0.0).astype(BF16)
        s = []
        for i in range(CA_WINDOW_BLOCKS):
            k0 = pl.multiple_of((first + i) * blk, blk)
            s.append(_dot_nt(qm, k_ref[pl.ds(k0, blk), :]) + u_ref[hh, w + i])
        m = s[0].max(axis=-1, keepdims=True)
        for i in range(1, CA_WINDOW_BLOCKS):
            m = jnp.maximum(m, s[i].max(axis=-1, keepdims=True))
        l = jnp.zeros((blk, 1), F32)
        acc = jnp.zeros((blk, LANES), F32)
        for i in range(CA_WINDOW_BLOCKS):
            k0 = pl.multiple_of((first + i) * blk, blk)
            pr = jnp.exp(s[i] - m)
            l = l + jnp.sum(pr, axis=-1, keepdims=True)
            acc = acc + _dot(pr.astype(BF16), v_ref[pl.ds(k0, blk), :])
        outs.append(acc / l)
    o_ref[...] = jnp.where(_head_lane_mask(0, outs[0].shape), outs[0], outs[1]).astype(o_ref.dtype)


def _ca_table(rel_bias, blk):
    qi = np.arange(blk)[:, None]
    k_rel = np.arange(CA_TABLE_BLOCKS * blk)[None, :] - (CA_WINDOW_BLOCKS - 1) * blk
    idx = np.clip(qi - k_rel, REL_MIN, REL_MAX) - REL_MIN
    dchunk = qi // CHUNK - np.floor_divide(k_rel, CHUNK)
    valid = (dchunk >= 0) & (dchunk <= CA_LEFT_CHUNKS)
    table = jnp.where(jnp.asarray(valid)[None], rel_bias.astype(F32)[:, idx], NEG_INF)
    H = rel_bias.shape[0]
    return table.reshape(H, blk, CA_TABLE_BLOCKS, blk).transpose(0, 2, 1, 3)


def _chunk_attention(z, table, base):
    B, S, _ = z.shape
    blk = ATT_BLOCK
    n_pair = CA_WIDTH // LANES
    return pl.pallas_call(
        functools.partial(_ca_kernel, blk=blk),
        out_shape=jax.ShapeDtypeStruct((B, S, CA_WIDTH), BF16),
        grid=(n_pair, B, S // blk),
        in_specs=[pl.BlockSpec((None, blk, LANES), lambda p, b, j: (b, j, base + p)),
                  pl.BlockSpec((None, S, LANES), lambda p, b, j: (b, 0, base + n_pair + p)),
                  pl.BlockSpec((None, S, LANES), lambda p, b, j: (b, 0, base + 2 * n_pair + p)),
                  pl.BlockSpec((HEAD_PAIR, CA_TABLE_BLOCKS, blk, blk), lambda p, b, j: (p, 0, 0, 0))],
        out_specs=pl.BlockSpec((None, blk, LANES), lambda p, b, j: (b, j, p)),
        compiler_params=_params(3),
        name="chunk_attention",
    )(z, z, z, table)


def _s5_kernel(u_ref, bbr_ref, bbi_ref, cc_ref, ar_ref, ai_ref, d_ref, wg_ref, o_ref,
               xr_ref, xi_ref, hr_ref, hi_ref, *, steps, batch):
    @pl.when(pl.program_id(0) == 0)
    def _init():
        hr_ref[...] = jnp.zeros_like(hr_ref)
        hi_ref[...] = jnp.zeros_like(hi_ref)

    u = u_ref[...]
    for i in range(SSM_SLABS):
        ui = u[:, i * LANES:(i + 1) * LANES]
        cols = slice(i * SSM_SLAB_STATE, (i + 1) * SSM_SLAB_STATE)
        xr_ref[:, cols] = _dot(ui, bbr_ref[i])
        xi_ref[:, cols] = _dot(ui, bbi_ref[i])

    for i in range(SSM_SLABS):
        cols = slice(i * SSM_SLAB_STATE, (i + 1) * SSM_SLAB_STATE)
        ar = ar_ref[:, cols]
        ai = ai_ref[:, cols]

        def body(t, carry, cols=cols, ar=ar, ai=ai):
            hr, hi = carry
            r0 = pl.multiple_of(t * batch, batch)
            nhr = ar * hr - ai * hi + xr_ref[pl.ds(r0, batch), cols]
            nhi = ar * hi + ai * hr + xi_ref[pl.ds(r0, batch), cols]
            xr_ref[pl.ds(r0, batch), cols] = nhr
            xi_ref[pl.ds(r0, batch), cols] = nhi
            return nhr, nhi

        hr, hi = lax.fori_loop(0, steps, body, (hr_ref[:, cols], hi_ref[:, cols]), unroll=8)
        hr_ref[:, cols] = hr
        hi_ref[:, cols] = hi

    ys = []
    for i in range(SSM_SLABS):
        cols = slice(i * SSM_SLAB_STATE, (i + 1) * SSM_SLAB_STATE)
        h = jnp.concatenate([xr_ref[:, cols].astype(BF16), xi_ref[:, cols].astype(BF16)], axis=1)
        ys.append(_dot(h, cc_ref[i]))
    y = jnp.concatenate(ys, axis=1) + d_ref[...] * u.astype(F32)
    z = _dot(jax.nn.gelu(y).astype(BF16), wg_ref[...])
    half = z.shape[1] // 2
    o_ref[...] = (z[:, :half] * jax.nn.sigmoid(z[:, half:])).astype(o_ref.dtype)


def _s5_branch(su, bbr, bbi, cc, ar, ai, d, w_glu, batch):
    rows, width = su.shape
    S = rows // batch
    steps = min(SSM_TIME_BLOCK, S)
    tr = steps * batch
    n_state = SSM_GROUPS * SSM_STATE
    d_out = w_glu.shape[1] // 2
    full = lambda a: pl.BlockSpec(a.shape, lambda t: (0,) * a.ndim)
    return pl.pallas_call(
        functools.partial(_s5_kernel, steps=steps, batch=batch),
        out_shape=jax.ShapeDtypeStruct((rows, d_out), BF16),
        grid=(S // steps,),
        in_specs=[pl.BlockSpec((tr, width), lambda t: (t, 0)),
                  full(bbr), full(bbi), full(cc), full(ar), full(ai), full(d), full(w_glu)],
        out_specs=pl.BlockSpec((tr, d_out), lambda t: (t, 0)),
        scratch_shapes=[pltpu.VMEM((tr, n_state), F32), pltpu.VMEM((tr, n_state), F32),
                        pltpu.VMEM((batch, n_state), F32), pltpu.VMEM((batch, n_state), F32)],
        compiler_params=_params(1),
        name="s5_branch",
    )(su, bbr, bbi, cc, ar, ai, d, w_glu)


def _s5_params(lam_re, lam_im, log_dt, b_re, b_im, c_re, c_im, batch):
    lr = jnp.minimum(lam_re.astype(F32), -1e-4)
    li = lam_im.astype(F32)
    dt = jnp.exp(log_dt.astype(F32))[:, None]
    mag = jnp.exp(lr * dt)
    ar = mag * jnp.cos(li * dt)
    ai = mag * jnp.sin(li * dt)
    den = lr * lr + li * li
    gr = ((ar - 1.0) * lr + ai * li) / den
    gi = (ai * lr - (ar - 1.0) * li) / den
    br, bi = b_re.astype(F32), b_im.astype(F32)
    bbr = gr[..., None] * br - gi[..., None] * bi
    bbi = gr[..., None] * bi + gi[..., None] * br
    eye = jnp.eye(SSM_SLAB_GROUPS, dtype=F32)

    def in_map(bb):
        bb = bb.reshape(SSM_SLABS, SSM_SLAB_GROUPS, SSM_STATE, SSM_GROUP)
        return jnp.einsum('sgpc,gh->sgchp', bb, eye).reshape(SSM_SLABS, LANES, SSM_SLAB_STATE).astype(BF16)

    def out_map(c):
        c = c.astype(F32).reshape(SSM_SLABS, SSM_SLAB_GROUPS, SSM_GROUP, SSM_STATE)
        return jnp.einsum('sgcp,gh->sgphc', c, eye).reshape(SSM_SLABS, SSM_SLAB_STATE, LANES)

    cc = jnp.concatenate([out_map(c_re), -out_map(c_im)], axis=1).astype(BF16)
    n_state = SSM_GROUPS * SSM_STATE
    ar_b = jnp.broadcast_to(ar.reshape(1, n_state), (batch, n_state))
    ai_b = jnp.broadcast_to(ai.reshape(1, n_state), (batch, n_state))
    return in_map(bbr), in_map(bbi), cc, ar_b, ai_b


def _merge_kernel(x_ref, g_ref, ya_ref, yb_ref, yc_ref, wa_ref, wc_ref, wo_ref, lg_ref, lb_ref,
                  of_ref, ob_ref, *, alpha):
    d = x_ref.shape[1]
    g = g_ref[...].astype(F32)
    merged = (jax.nn.sigmoid(g[:, :d]) * _dot(ya_ref[...], wa_ref[...])
              + jax.nn.sigmoid(g[:, d:2 * d]) * yb_ref[...].astype(F32)
              + jax.nn.sigmoid(g[:, 2 * d:]) * _dot(yc_ref[...], wc_ref[...]))
    h = _dot(merged.astype(BF16), wo_ref[...])
    y = _layer_norm(alpha * x_ref[...] + h, lg_ref[...], lb_ref[...])
    of_ref[...] = y
    ob_ref[...] = y.astype(BF16)


def _merge(x, z, ya, yb_tm, yc, w_fox_o, w_ca_o, w_o, lg, lb, alpha, tm):
    B, S, D = x.shape
    tm = min(tm, S)
    row = lambda w: pl.BlockSpec((None, tm, w), lambda b, i: (b, i, 0))
    full = lambda a: pl.BlockSpec(a.shape, lambda b, i: (0,) * a.ndim)
    return pl.pallas_call(
        functools.partial(_merge_kernel, alpha=alpha),
        out_shape=(jax.ShapeDtypeStruct((B, S, D), F32), jax.ShapeDtypeStruct((B, S, D), BF16)),
        grid=(B, S // tm),
        in_specs=[row(D), row(3 * D), row(ya.shape[2]),
                  pl.BlockSpec((tm, D), lambda b, i: (i, b)),
                  row(yc.shape[2]), full(w_fox_o), full(w_ca_o), full(w_o), full(lg), full(lb)],
        out_specs=(row(D), row(D)),
        compiler_params=_params(2),
        name="merge_proj_ln",
    )(x, z, ya, yb_tm, yc, w_fox_o, w_ca_o, w_o, lg, lb)


def _xattn_kernel(x_ref, xb_ref, kv_ref, wq_ref, wo_ref, lg_ref, lb_ref, of_ref, ob_ref, *, alpha):
    d = x_ref.shape[1]
    dh = d // XA_HEADS
    q = (_dot(xb_ref[...], wq_ref[...]) * (dh ** -0.5)).astype(BF16)
    outs = []
    for h in range(XA_HEADS):
        s = _dot_nt(q[:, h * dh:(h + 1) * dh], kv_ref[:, h * dh:(h + 1) * dh])
        m = jnp.max(s, axis=-1, keepdims=True)
        pr = jnp.exp(s - m)
        l = jnp.sum(pr, axis=-1, keepdims=True)
        outs.append((_dot(pr.astype(BF16), kv_ref[:, d + h * dh:d + (h + 1) * dh]) / l).astype(BF16))
    hout = _dot(jnp.concatenate(outs, axis=1), wo_ref[...])
    y = _layer_norm(alpha * x_ref[...] + hout, lg_ref[...], lb_ref[...])
    of_ref[...] = y
    ob_ref[...] = y.astype(BF16)


def _cross_attention(x, xb, kv, wq, wo, lg, lb, alpha, tm):
    B, S, D = x.shape
    M = kv.shape[1]
    tm = min(tm, S)
    row = pl.BlockSpec((None, tm, D), lambda b, i: (b, i, 0))
    full = lambda a: pl.BlockSpec(a.shape, lambda b, i: (0,) * a.ndim)
    return pl.pallas_call(
        functools.partial(_xattn_kernel, alpha=alpha),
        out_shape=(jax.ShapeDtypeStruct((B, S, D), F32), jax.ShapeDtypeStruct((B, S, D), BF16)),
        grid=(B, S // tm),
        in_specs=[row, row, pl.BlockSpec((None, M, 2 * D), lambda b, i: (b, 0, 0)),
                  full(wq), full(wo), full(lg), full(lb)],
        out_specs=(row, row),
        compiler_params=_params(2),
        name="cross_attention_ln",
    )(x, xb, kv, wq, wo, lg, lb)


def _ffn_up_kernel(x_ref, wa_ref, wg_ref, cw_ref, cb_ref, o_ref, tail_ref, *, tiles_per_seq):
    i = pl.program_id(1)
    tm = x_ref.shape[0]
    x = x_ref[...]
    a = _dot(x, wa_ref[...])
    g = _dot(x, wg_ref[...])

    @pl.when(i % tiles_per_seq == 0)
    def _seq_start():
        tail_ref[...] = jnp.zeros_like(tail_ref)

    tail = tail_ref[...]
    row = lax.broadcasted_iota(jnp.int32, a.shape, 0)
    a1 = jnp.where(row == 0, tail[7:8, :], pltpu.roll(a, 1, 0))
    a2 = jnp.where(row == 0, tail[6:7, :], jnp.where(row == 1, tail[7:8, :], pltpu.roll(a, 2, 0)))
    tail_ref[...] = a[tm - 8:, :]
    cw = cw_ref[...]
    conv = cw[0:1, :] * a2 + cw[1:2, :] * a1 + cw[2:3, :] * a + cb_ref[...]
    o_ref[...] = (jax.nn.gelu(conv) * g).astype(o_ref.dtype)


def _ffn_up(xb, w_a, w_g, conv_w, conv_b, tm, n_col):
    B, S, D = xb.shape
    F = w_a.shape[1]
    tm = min(tm, S)
    tn = F // n_col
    tiles_per_seq = S // tm
    x2 = xb.reshape(B * S, D)
    out = pl.pallas_call(
        functools.partial(_ffn_up_kernel, tiles_per_seq=tiles_per_seq),
        out_shape=jax.ShapeDtypeStruct((B * S, F), BF16),
        grid=(n_col, B * S // tm),
        in_specs=[pl.BlockSpec((tm, D), lambda n, i: (i, 0)),
                  pl.BlockSpec((D, tn), lambda n, i: (0, n)),
                  pl.BlockSpec((D, tn), lambda n, i: (0, n)),
                  pl.BlockSpec((CONV_WIDTH, tn), lambda n, i: (0, n)),
                  pl.BlockSpec((1, tn), lambda n, i: (0, n))],
        out_specs=pl.BlockSpec((tm, tn), lambda n, i: (i, n)),
        scratch_shapes=[pltpu.VMEM((8, tn), F32)],
        compiler_params=_params(2),
        name="ffn_up_conv",
    )(x2, w_a, w_g, conv_w, conv_b)
    return out.reshape(B, S, F)


def _ffn_down_kernel(x_ref, a_ref, w_ref, lg_ref, lb_ref, of_ref, ob_ref, *, alpha):
    h = _dot(a_ref[...], w_ref[...])
    y = _layer_norm(alpha * x_ref[...] + h, lg_ref[...], lb_ref[...])
    of_ref[...] = y
    ob_ref[...] = y.astype(BF16)


def _ffn_down(x, act, w_down, lg, lb, alpha, tm):
    B, S, D = x.shape
    F = act.shape[2]
    tm = min(tm, S)
    row = lambda w: pl.BlockSpec((None, tm, w), lambda b, i: (b, i, 0))
    full = lambda a: pl.BlockSpec(a.shape, lambda b, i: (0,) * a.ndim)
    return pl.pallas_call(
        functools.partial(_ffn_down_kernel, alpha=alpha),
        out_shape=(jax.ShapeDtypeStruct((B, S, D), F32), jax.ShapeDtypeStruct((B, S, D), BF16)),
        grid=(B, S // tm),
        in_specs=[row(D), row(F), full(w_down), full(lg), full(lb)],
        out_specs=(row(D), row(D)),
        compiler_params=_params(2),
        name="ffn_down_ln",
    )(x, act, w_down, lg, lb)


def _split_in_proj(w_in, b_in, d_model):
    o = np.cumsum([0, FOX_WIDTH, FOX_WIDTH, FOX_WIDTH, FOX_HEADS, SSM_WIDTH,
                   CA_WIDTH, CA_WIDTH, CA_WIDTH, 3 * d_model])
    fqkv, ff, su, cqkv, gates = (slice(o[0], o[3]), slice(o[3], o[4]), slice(o[4], o[5]),
                                 slice(o[5], o[8]), slice(o[8], o[9]))
    cols = lambda s: (w_in[:, s], b_in[s])
    cat = lambda a, b: (jnp.concatenate([a[0], b[0]], axis=1), jnp.concatenate([a[1], b[1]]))
    w_main, b_main = cat(cols(gates), cat(cols(fqkv), cols(cqkv)))
    w_su, b_su = cols(su)
    w_ff, b_ff = cols(ff)
    pad = LANES - FOX_HEADS
    w_ff = jnp.pad(w_ff, ((0, 0), (0, pad)))
    b_ff = jnp.pad(b_ff, (0, pad))
    prep = lambda w, b: (w.astype(BF16), b.astype(F32)[None, :])
    return prep(w_main, b_main), prep(w_su, b_su), prep(w_ff, b_ff)


def kernel(x, mem, w_in, b_in, ssm_lambda_re, ssm_lambda_im, ssm_log_dt, ssm_b_re, ssm_b_im, ssm_c_re, ssm_c_im, ssm_d, ca_rel_bias, w_fox_o, w_ssm_glu, w_ca_o, w_o, xa_wq, xa_wkv, xa_wo, ffn_w_up, ffn_conv_w, ffn_conv_b, ffn_w_down, ln_g, ln_b):
    B, S, D = x.shape
    depth = w_in.shape[0]
    d_ff = ffn_w_down.shape[1]
    alpha = float((2 * depth) ** 0.25)
    mem_b = mem.astype(BF16)
    xf = x.astype(F32)
    xb = x.astype(BF16)
    zero_bias = jnp.zeros((1, 2 * D), F32)
    for l in range(depth):
        (w_main, b_main), (w_su, b_su), (w_ff, b_ff) = _split_in_proj(w_in[l], b_in[l], D)
        z = _matmul(xb, w_main, b_main, BF16, tm=1024, tn=1024, name="in_proj")
        su = _matmul(xb, w_su, b_su, BF16, tm=1024, tn=512, time_major=True, name="in_proj_ssm")
        ff = _matmul(xb, w_ff, b_ff, F32, tm=1024, tn=LANES, name="in_proj_forget")
        cum = _forget_cumsum(ff)
        fox_base = 3 * D // LANES
        ya = _fox_attention(z, cum, fox_base)
        yc = _chunk_attention(z, _ca_table(ca_rel_bias[l], ATT_BLOCK), fox_base + 3 * FOX_WIDTH // LANES)
        bbr, bbi, cc, ar, ai = _s5_params(ssm_lambda_re[l], ssm_lambda_im[l], ssm_log_dt[l],
                                          ssm_b_re[l], ssm_b_im[l], ssm_c_re[l], ssm_c_im[l], B)
        yb = _s5_branch(su.reshape(S * B, SSM_WIDTH), bbr, bbi, cc, ar, ai,
                        ssm_d[l].astype(F32)[None, :], w_ssm_glu[l].astype(BF16), B)
        ln = lambda k: (ln_g[l, k].astype(F32)[None, :], ln_b[l, k].astype(F32)[None, :])
        xf, xb = _merge(xf, z, ya, yb.reshape(S, B * D), yc, w_fox_o[l].astype(BF16),
                        w_ca_o[l].astype(BF16), w_o[l].astype(BF16), *ln(0), alpha, tm=512)
        kv = _matmul(mem_b, xa_wkv[l].astype(BF16), zero_bias, BF16, tm=256, tn=1024, name="mem_kv")
        xf, xb = _cross_attention(xf, xb, kv, xa_wq[l].astype(BF16), xa_wo[l].astype(BF16),
                                  *ln(1), alpha, tm=512)
        w_up = ffn_w_up[l].astype(BF16)
        act = _ffn_up(xb, w_up[:, :d_ff], w_up[:, d_ff:], ffn_conv_w[l].astype(F32),
                      ffn_conv_b[l].astype(F32)[None, :], tm=512, n_col=2)
        xf, xb = _ffn_down(xf, act, ffn_w_down[l].astype(BF16), *ln(2), alpha, tm=512)
    return xf.astype(x.dtype)
```

```python
import functools
import math

import jax
import jax.numpy as jnp
import numpy as np
from jax import lax
from jax.experimental import pallas as pl
from jax.experimental.pallas import tpu as pltpu

F32 = jnp.float32
BF16 = jnp.bfloat16

CHUNK = 64
Q_BLOCK = 128
FOX_HEADS = 8
FOX_HEAD_DIM = 64
FOX_WIDTH = FOX_HEADS * FOX_HEAD_DIM
SSM_GROUP = 16
SSM_WIDTH = 512
SSM_GROUPS = SSM_WIDTH // SSM_GROUP
SSM_STATE = 64
CA_HEADS = 8
CA_HEAD_DIM = 64
CA_WIDTH = CA_HEADS * CA_HEAD_DIM
CA_LEFT_CHUNKS = 8
REL_MIN = -(CHUNK - 1)
REL_MAX = 4 * CHUNK
XA_HEADS = 4
CONV_WIDTH = 3
LN_EPS = 1e-5
NEG_INF = -1e30

LANES = 128
HEAD_PAIR = LANES // FOX_HEAD_DIM
ATT_BLOCK = 256
FOX_BLOCK = 512
CA_WINDOW_BLOCKS = 3
CA_TABLE_BLOCKS = 5
SSM_SLAB_GROUPS = LANES // SSM_GROUP
SSM_SLABS = SSM_GROUPS // SSM_SLAB_GROUPS
SSM_SLAB_STATE = SSM_SLAB_GROUPS * SSM_STATE
SSM_TIME_BLOCK = 128
VMEM_LIMIT = 56 * 1024 * 1024

_ARB = "arbitrary"


def _params(n_axes):
    return pltpu.CompilerParams(dimension_semantics=(_ARB,) * n_axes,
                                vmem_limit_bytes=VMEM_LIMIT)


def _layer_norm(v, g, b):
    mu = jnp.mean(v, axis=-1, keepdims=True)
    d = v - mu
    var = jnp.mean(d * d, axis=-1, keepdims=True)
    return d * lax.rsqrt(var + LN_EPS) * g + b


def _dot(a, b):
    return jnp.dot(a, b, preferred_element_type=F32)


def _dot_nt(a, b):
    return lax.dot_general(a, b, (((1,), (1,)), ((), ())), preferred_element_type=F32)


def _mm_kernel(x_ref, w_ref, b_ref, o_ref):
    o_ref[...] = (_dot(x_ref[...], w_ref[...]) + b_ref[...]).astype(o_ref.dtype)


def _matmul(x, w, bias, out_dtype, *, tm, tn, time_major=False, name="matmul"):
    B, S, K = x.shape
    N = w.shape[1]
    tm, tn = min(tm, S), min(tn, N)
    grid = (B, S // tm, N // tn)
    if time_major:
        out_shape = jax.ShapeDtypeStruct((S, B * N), out_dtype)
        out_spec = pl.BlockSpec((tm, tn), lambda b, i, j: (i, b * (N // tn) + j))
    else:
        out_shape = jax.ShapeDtypeStruct((B, S, N), out_dtype)
        out_spec = pl.BlockSpec((None, tm, tn), lambda b, i, j: (b, i, j))
    return pl.pallas_call(
        _mm_kernel,
        out_shape=out_shape,
        grid=grid,
        in_specs=[pl.BlockSpec((None, tm, K), lambda b, i, j: (b, i, 0)),
                  pl.BlockSpec((K, tn), lambda b, i, j: (0, j)),
                  pl.BlockSpec((1, tn), lambda b, i, j: (0, j))],
        out_specs=out_spec,
        compiler_params=_params(3),
        name=name,
    )(x, w, bias)


def _split3(c):
    hi = c.astype(BF16).astype(F32)
    r = c - hi
    mid = r.astype(BF16).astype(F32)
    lo = (r - mid).astype(BF16).astype(F32)
    return hi, mid, lo


def _cum_kernel(ff_ref, o_ref, *, blk):
    S = ff_ref.shape[0]
    row = lax.broadcasted_iota(jnp.int32, (blk, blk), 0)
    col = lax.broadcasted_iota(jnp.int32, (blk, blk), 1)
    tri = jnp.where(row >= col, 1.0, 0.0).astype(BF16)
    carry = jnp.zeros((1, LANES), F32)
    for i in range(S // blk):
        x = ff_ref[i * blk:(i + 1) * blk, :]
        ls = jnp.minimum(x, 0.0) - jnp.log(1.0 + jnp.exp(-jnp.abs(x)))
        hi, mid, lo = _split3(ls)
        cs = (_dot(tri, hi.astype(BF16)) + _dot(tri, mid.astype(BF16))
              + _dot(tri, lo.astype(BF16))) + carry
        o_ref[i * blk:(i + 1) * blk, :] = cs
        carry = cs[blk - 1:blk, :]


def _forget_cumsum(ff):
    B, S, L = ff.shape
    return pl.pallas_call(
        functools.partial(_cum_kernel, blk=min(ATT_BLOCK, S)),
        out_shape=jax.ShapeDtypeStruct((B, S, L), F32),
        grid=(B,),
        in_specs=[pl.BlockSpec((None, S, L), lambda b: (b, 0, 0))],
        out_specs=pl.BlockSpec((None, S, L), lambda b: (b, 0, 0)),
        compiler_params=_params(1),
        name="forget_cumsum",
    )(ff)


def _head_lane_mask(j, shape):
    lane = lax.broadcasted_iota(jnp.int32, shape, len(shape) - 1)
    lo = j * FOX_HEAD_DIM
    return (lane >= lo) & (lane < lo + FOX_HEAD_DIM)


def _decay_lanes(cum_col, j, rows, query_side):
    a0 = (1 - j) * FOX_HEAD_DIM
    lane = lax.broadcasted_iota(jnp.int32, (rows, LANES), 1) - a0
    hi, mid, lo = _split3(cum_col)
    if query_side:
        vals = (hi, mid, lo, 1.0, 1.0, 1.0)
    else:
        vals = (1.0, 1.0, 1.0, -hi, -mid, -lo)
    out = jnp.zeros((rows, LANES), F32)
    for i, v in enumerate(vals):
        out = jnp.where(lane == i, v, out)
    return out


def _cum_column(cum_blk, h):
    lane = lax.broadcasted_iota(jnp.int32, cum_blk.shape, 1)
    return jnp.sum(jnp.where(lane == h, cum_blk, 0.0), axis=1, keepdims=True)


def _fox_kernel(q_ref, k_ref, v_ref, cum_ref, o_ref, kaug_ref, vaug_ref, s_ref, ml_ref, mb_ref, acc_ref,
                *, blk):
    p = pl.program_id(1)
    qi = pl.program_id(2)
    S = k_ref.shape[0]

    @pl.when(qi == 0)
    def _build_keys():
        for r in range(S // blk):
            rows = slice(r * blk, (r + 1) * blk)
            kf = k_ref[rows, :].astype(F32)
            vf = v_ref[rows, :].astype(F32)
            cum_blk = cum_ref[rows, :]
            for j in range(HEAD_PAIR):
                own = _head_lane_mask(j, kf.shape)
                col = _cum_column(cum_blk, HEAD_PAIR * p + j)
                aug = _decay_lanes(col, j, blk, query_side=False)
                kaug_ref[j, rows, :] = jnp.where(own, kf, aug).astype(BF16)
                vaug_ref[j, rows, :] = jnp.where(own, vf, 1.0).astype(BF16)

    q0 = pl.multiple_of(qi * blk, blk)
    qf = q_ref[...].astype(F32) * (FOX_HEAD_DIM ** -0.5)
    cum_q = cum_ref[pl.ds(q0, blk), :]
    qaug = []
    for j in range(HEAD_PAIR):
        col = _cum_column(cum_q, HEAD_PAIR * p + j)
        aug = _decay_lanes(col, j, blk, query_side=True)
        qaug.append(jnp.where(_head_lane_mask(j, qf.shape), qf, aug).astype(BF16))

    ml_ref[...] = jnp.full(ml_ref.shape, NEG_INF, F32)
    acc_ref[...] = jnp.zeros(acc_ref.shape, F32)

    def scores(c, masked):
        k0 = pl.multiple_of(c * blk, blk)
        for j in range(HEAD_PAIR):
            s = _dot_nt(qaug[j], kaug_ref[j, pl.ds(k0, blk), :])
            if masked:
                row = lax.broadcasted_iota(jnp.int32, (blk, blk), 0)
                colk = lax.broadcasted_iota(jnp.int32, (blk, blk), 1)
                s = jnp.where(row >= colk, s, NEG_INF)
            s_ref[j, c] = s
            ml = ml_ref[j]
            for g in range(blk // LANES):
                ml = jnp.maximum(ml, s[:, g * LANES:(g + 1) * LANES])
            ml_ref[j] = ml

    def _scores_body(c, carry):
        scores(c, False)
        return carry

    lax.fori_loop(0, qi, _scores_body, 0)
    scores(qi, True)

    for j in range(HEAD_PAIR):
        m = jnp.max(ml_ref[j], axis=1, keepdims=True)
        mb_ref[j] = jnp.broadcast_to(m, (blk, LANES))

    def _values_body(c, carry):
        k0 = pl.multiple_of(c * blk, blk)
        for j in range(HEAD_PAIR):
            mb = mb_ref[j]
            pr = jnp.exp(s_ref[j, c] - jnp.concatenate([mb] * (blk // LANES), axis=1))
            acc_ref[j] += _dot(pr.astype(BF16), vaug_ref[j, pl.ds(k0, blk), :])
        return carry

    lax.fori_loop(0, qi + 1, _values_body, 0)

    outs = []
    for j in range(HEAD_PAIR):
        acc = acc_ref[j]
        outs.append(acc / pltpu.roll(acc, FOX_HEAD_DIM, 1))
    o_ref[...] = jnp.where(_head_lane_mask(0, outs[0].shape), outs[0], outs[1]).astype(o_ref.dtype)


def _fox_attention(z, cum, base):
    B, S, _ = z.shape
    blk = min(FOX_BLOCK, S)
    n_pair = FOX_WIDTH // LANES
    return pl.pallas_call(
        functools.partial(_fox_kernel, blk=blk),
        out_shape=jax.ShapeDtypeStruct((B, S, FOX_WIDTH), BF16),
        grid=(B, n_pair, S // blk),
        in_specs=[pl.BlockSpec((None, blk, LANES), lambda b, p, i: (b, i, base + p)),
                  pl.BlockSpec((None, S, LANES), lambda b, p, i: (b, 0, base + n_pair + p)),
                  pl.BlockSpec((None, S, LANES), lambda b, p, i: (b, 0, base + 2 * n_pair + p)),
                  pl.BlockSpec((None, S, LANES), lambda b, p, i: (b, 0, 0))],
        out_specs=pl.BlockSpec((None, blk, LANES), lambda b, p, i: (b, i, p)),
        scratch_shapes=[pltpu.VMEM((HEAD_PAIR, S, LANES), BF16),
                        pltpu.VMEM((HEAD_PAIR, S, LANES), BF16),
                        pltpu.VMEM((HEAD_PAIR, S // blk, blk, blk), F32),
                        pltpu.VMEM((HEAD_PAIR, blk, LANES), F32),
                        pltpu.VMEM((HEAD_PAIR, blk, LANES), F32),
                        pltpu.VMEM((HEAD_PAIR, blk, LANES), F32)],
        compiler_params=_params(3),
        name="fox_attention",
    )(z, z, z, cum)


def _ca_kernel(q_ref, k_ref, v_ref, u_ref, o_ref, *, blk):
    j = pl.program_id(2)
    left = CA_WINDOW_BLOCKS - 1
    first = jnp.maximum(j - left, 0)
    w = left - jnp.minimum(j, left)
    qf = q_ref[...].astype(F32) * (CA_HEAD_DIM ** -0.5)
    outs = []
    for hh in range(HEAD_PAIR):
        qm = jnp.where(_head_lane_mask(hh, qf.shape), qf, 0.0).astype(BF16)
        s = []
        for i in range(CA_WINDOW_BLOCKS):
            k0 = pl.multiple_of((first + i) * blk, blk)
            s.append(_dot_nt(qm, k_ref[pl.ds(k0, blk), :]) + u_ref[hh, w + i])
        m = s[0].max(axis=-1, keepdims=True)
        for i in range(1, CA_WINDOW_BLOCKS):
            m = jnp.maximum(m, s[i].max(axis=-1, keepdims=True))
        l = jnp.zeros((blk, 1), F32)
        acc = jnp.zeros((blk, LANES), F32)
        for i in range(CA_WINDOW_BLOCKS):
            k0 = pl.multiple_of((first + i) * blk, blk)
            pr = jnp.exp(s[i] - m)
            l = l + jnp.sum(pr, axis=-1, keepdims=True)
            acc = acc + _dot(pr.astype(BF16), v_ref[pl.ds(k0, blk), :])
        outs.append(acc / l)
    o_ref[...] = jnp.where(_head_lane_mask(0, outs[0].shape), outs[0], outs[1]).astype(o_ref.dtype)


def _ca_table(rel_bias, blk):
    H = rel_bias.shape[0]
    n_m = CA_TABLE_BLOCKS * blk
    shift = (CA_WINDOW_BLOCKS - 1) * blk
    qi = np.arange(blk)[:, None]
    k_rel = np.arange(n_m)[None, :] - shift
    dchunk = qi // CHUNK - np.floor_divide(k_rel, CHUNK)
    valid = (dchunk >= 0) & (dchunk <= CA_LEFT_CHUNKS)
    period = n_m + blk
    x = np.arange(period)
    x = np.where(x >= n_m, x - period, x)
    vec = rel_bias.astype(F32)[:, np.clip(shift - x, REL_MIN, REL_MAX) - REL_MIN]
    toep = jnp.tile(vec, (1, blk))[:, :blk * (period - 1)].reshape(H, blk, period - 1)[:, :, :n_m]
    table = jnp.where(jnp.asarray(valid)[None], toep, NEG_INF)
    return table.reshape(H, blk, CA_TABLE_BLOCKS, blk).transpose(0, 2, 1, 3)


def _chunk_attention(z, table, base):
    B, S, _ = z.shape
    blk = ATT_BLOCK
    n_pair = CA_WIDTH // LANES
    return pl.pallas_call(
        functools.partial(_ca_kernel, blk=blk),
        out_shape=jax.ShapeDtypeStruct((B, S, CA_WIDTH), BF16),
        grid=(n_pair, B, S // blk),
        in_specs=[pl.BlockSpec((None, blk, LANES), lambda p, b, j: (b, j, base + p)),
                  pl.BlockSpec((None, S, LANES), lambda p, b, j: (b, 0, base + n_pair + p)),
                  pl.BlockSpec((None, S, LANES), lambda p, b, j: (b, 0, base + 2 * n_pair + p)),
                  pl.BlockSpec((HEAD_PAIR, CA_TABLE_BLOCKS, blk, blk), lambda p, b, j: (p, 0, 0, 0))],
        out_specs=pl.BlockSpec((None, blk, LANES), lambda p, b, j: (b, j, p)),
        compiler_params=_params(3),
        name="chunk_attention",
    )(z, z, z, table)


def _s5_kernel(u_ref, bbr_ref, bbi_ref, cc_ref, ar_ref, ai_ref, d_ref, wg_ref, o_ref,
               xr_ref, xi_ref, hr_ref, hi_ref, *, steps, batch):
    @pl.when(pl.program_id(0) == 0)
    def _init():
        hr_ref[...] = jnp.zeros_like(hr_ref)
        hi_ref[...] = jnp.zeros_like(hi_ref)

    u = u_ref[...]
    for i in range(SSM_SLABS):
        ui = u[:, i * LANES:(i + 1) * LANES]
        cols = slice(i * SSM_SLAB_STATE, (i + 1) * SSM_SLAB_STATE)
        xr_ref[:, cols] = _dot(ui, bbr_ref[i])
        xi_ref[:, cols] = _dot(ui, bbi_ref[i])

    for i in range(SSM_SLABS):
        cols = slice(i * SSM_SLAB_STATE, (i + 1) * SSM_SLAB_STATE)
        ar = ar_ref[:, cols]
        ai = ai_ref[:, cols]

        def body(t, carry, cols=cols, ar=ar, ai=ai):
            hr, hi = carry
            r0 = pl.multiple_of(t * batch, batch)
            nhr = ar * hr - ai * hi + xr_ref[pl.ds(r0, batch), cols]
            nhi = ar * hi + ai * hr + xi_ref[pl.ds(r0, batch), cols]
            xr_ref[pl.ds(r0, batch), cols] = nhr
            xi_ref[pl.ds(r0, batch), cols] = nhi
            return nhr, nhi

        hr, hi = lax.fori_loop(0, steps, body, (hr_ref[:, cols], hi_ref[:, cols]), unroll=8)
        hr_ref[:, cols] = hr
        hi_ref[:, cols] = hi

    ys = []
    for i in range(SSM_SLABS):
        cols = slice(i * SSM_SLAB_STATE, (i + 1) * SSM_SLAB_STATE)
        h = jnp.concatenate([xr_ref[:, cols].astype(BF16), xi_ref[:, cols].astype(BF16)], axis=1)
        ys.append(_dot(h, cc_ref[i]))
    y = jnp.concatenate(ys, axis=1) + d_ref[...] * u.astype(F32)
    z = _dot(jax.nn.gelu(y).astype(BF16), wg_ref[...])
    half = z.shape[1] // 2
    o_ref[...] = (z[:, :half] * jax.nn.sigmoid(z[:, half:])).astype(o_ref.dtype)


def _s5_branch(su, bbr, bbi, cc, ar, ai, d, w_glu, batch):
    rows, width = su.shape
    S = rows // batch
    steps = min(SSM_TIME_BLOCK, S)
    tr = steps * batch
    n_state = SSM_GROUPS * SSM_STATE
    d_out = w_glu.shape[1] // 2
    full = lambda a: pl.BlockSpec(a.shape, lambda t: (0,) * a.ndim)
    return pl.pallas_call(
        functools.partial(_s5_kernel, steps=steps, batch=batch),
        out_shape=jax.ShapeDtypeStruct((rows, d_out), BF16),
        grid=(S // steps,),
        in_specs=[pl.BlockSpec((tr, width), lambda t: (t, 0)),
                  full(bbr), full(bbi), full(cc), full(ar), full(ai), full(d), full(w_glu)],
        out_specs=pl.BlockSpec((tr, d_out), lambda t: (t, 0)),
        scratch_shapes=[pltpu.VMEM((tr, n_state), F32), pltpu.VMEM((tr, n_state), F32),
                        pltpu.VMEM((batch, n_state), F32), pltpu.VMEM((batch, n_state), F32)],
        compiler_params=_params(1),
        name="s5_branch",
    )(su, bbr, bbi, cc, ar, ai, d, w_glu)


def _s5_params(lam_re, lam_im, log_dt, b_re, b_im, c_re, c_im, batch):
    lr = jnp.minimum(lam_re.astype(F32), -1e-4)
    li = lam_im.astype(F32)
    dt = jnp.exp(log_dt.astype(F32))[:, None]
    mag = jnp.exp(lr * dt)
    ar = mag * jnp.cos(li * dt)
    ai = mag * jnp.sin(li * dt)
    den = lr * lr + li * li
    gr = ((ar - 1.0) * lr + ai * li) / den
    gi = (ai * lr - (ar - 1.0) * li) / den
    br, bi = b_re.astype(F32), b_im.astype(F32)
    bbr = gr[..., None] * br - gi[..., None] * bi
    bbi = gr[..., None] * bi + gi[..., None] * br
    eye = jnp.eye(SSM_SLAB_GROUPS, dtype=F32)

    def in_map(bb):
        bb = bb.reshape(SSM_SLABS, SSM_SLAB_GROUPS, SSM_STATE, SSM_GROUP)
        return jnp.einsum('sgpc,gh->sgchp', bb, eye).reshape(SSM_SLABS, LANES, SSM_SLAB_STATE).astype(BF16)

    def out_map(c):
        c = c.astype(F32).reshape(SSM_SLABS, SSM_SLAB_GROUPS, SSM_GROUP, SSM_STATE)
        return jnp.einsum('sgcp,gh->sgphc', c, eye).reshape(SSM_SLABS, SSM_SLAB_STATE, LANES)

    cc = jnp.concatenate([out_map(c_re), -out_map(c_im)], axis=1).astype(BF16)
    n_state = SSM_GROUPS * SSM_STATE
    ar_b = jnp.broadcast_to(ar.reshape(1, n_state), (batch, n_state))
    ai_b = jnp.broadcast_to(ai.reshape(1, n_state), (batch, n_state))
    return in_map(bbr), in_map(bbi), cc, ar_b, ai_b


def _merge_kernel(x_ref, g_ref, ya_ref, yb_ref, yc_ref, wa_ref, wc_ref, wo_ref, lg_ref, lb_ref,
                  of_ref, ob_ref, *, alpha):
    d = x_ref.shape[1]
    g = g_ref[...].astype(F32)
    merged = (jax.nn.sigmoid(g[:, :d]) * _dot(ya_ref[...], wa_ref[...])
              + jax.nn.sigmoid(g[:, d:2 * d]) * yb_ref[...].astype(F32)
              + jax.nn.sigmoid(g[:, 2 * d:]) * _dot(yc_ref[...], wc_ref[...]))
    h = _dot(merged.astype(BF16), wo_ref[...])
    y = _layer_norm(alpha * x_ref[...] + h, lg_ref[...], lb_ref[...])
    of_ref[...] = y
    ob_ref[...] = y.astype(BF16)


def _merge(x, z, ya, yb_tm, yc, w_fox_o, w_ca_o, w_o, lg, lb, alpha, tm):
    B, S, D = x.shape
    tm = min(tm, S)
    row = lambda w: pl.BlockSpec((None, tm, w), lambda b, i: (b, i, 0))
    full = lambda a: pl.BlockSpec(a.shape, lambda b, i: (0,) * a.ndim)
    return pl.pallas_call(
        functools.partial(_merge_kernel, alpha=alpha),
        out_shape=(jax.ShapeDtypeStruct((B, S, D), F32), jax.ShapeDtypeStruct((B, S, D), BF16)),
        grid=(B, S // tm),
        in_specs=[row(D), row(3 * D), row(ya.shape[2]),
                  pl.BlockSpec((tm, D), lambda b, i: (i, b)),
                  row(yc.shape[2]), full(w_fox_o), full(w_ca_o), full(w_o), full(lg), full(lb)],
        out_specs=(row(D), row(D)),
        compiler_params=_params(2),
        name="merge_proj_ln",
    )(x, z, ya, yb_tm, yc, w_fox_o, w_ca_o, w_o, lg, lb)


def _xattn_kernel(x_ref, xb_ref, kv_ref, wq_ref, wo_ref, lg_ref, lb_ref, of_ref, ob_ref, *, alpha):
    d = x_ref.shape[1]
    dh = d // XA_HEADS
    q = (_dot(xb_ref[...], wq_ref[...]) * (dh ** -0.5)).astype(BF16)
    outs = []
    for h in range(XA_HEADS):
        s = _dot_nt(q[:, h * dh:(h + 1) * dh], kv_ref[:, h * dh:(h + 1) * dh])
        m = jnp.max(s, axis=-1, keepdims=True)
        pr = jnp.exp(s - m)
        l = jnp.sum(pr, axis=-1, keepdims=True)
        outs.append((_dot(pr.astype(BF16), kv_ref[:, d + h * dh:d + (h + 1) * dh]) / l).astype(BF16))
    hout = _dot(jnp.concatenate(outs, axis=1), wo_ref[...])
    y = _layer_norm(alpha * x_ref[...] + hout, lg_ref[...], lb_ref[...])
    of_ref[...] = y
    ob_ref[...] = y.astype(BF16)


def _cross_attention(x, xb, kv, wq, wo, lg, lb, alpha, tm):
    B, S, D = x.shape
    M = kv.shape[1]
    tm = min(tm, S)
    row = pl.BlockSpec((None, tm, D), lambda b, i: (b, i, 0))
    full = lambda a: pl.BlockSpec(a.shape, lambda b, i: (0,) * a.ndim)
    return pl.pallas_call(
        functools.partial(_xattn_kernel, alpha=alpha),
        out_shape=(jax.ShapeDtypeStruct((B, S, D), F32), jax.ShapeDtypeStruct((B, S, D), BF16)),
        grid=(B, S // tm),
        in_specs=[row, row, pl.BlockSpec((None, M, 2 * D), lambda b, i: (b, 0, 0)),
                  full(wq), full(wo), full(lg), full(lb)],
        out_specs=(row, row),
        compiler_params=_params(2),
        name="cross_attention_ln",
    )(x, xb, kv, wq, wo, lg, lb)


def _ffn_up_kernel(x_ref, wa_ref, wg_ref, cw_ref, cb_ref, o_ref, tail_ref, *, tiles_per_seq):
    i = pl.program_id(1)
    tm = x_ref.shape[0]
    x = x_ref[...]
    a = _dot(x, wa_ref[...])
    g = _dot(x, wg_ref[...])

    @pl.when(i % tiles_per_seq == 0)
    def _seq_start():
        tail_ref[...] = jnp.zeros_like(tail_ref)

    tail = tail_ref[...]
    row = lax.broadcasted_iota(jnp.int32, a.shape, 0)
    a1 = jnp.where(row == 0, tail[7:8, :], pltpu.roll(a, 1, 0))
    a2 = jnp.where(row == 0, tail[6:7, :], jnp.where(row == 1, tail[7:8, :], pltpu.roll(a, 2, 0)))
    tail_ref[...] = a[tm - 8:, :]
    cw = cw_ref[...]
    conv = cw[0:1, :] * a2 + cw[1:2, :] * a1 + cw[2:3, :] * a + cb_ref[...]
    o_ref[...] = (jax.nn.gelu(conv) * g).astype(o_ref.dtype)


def _ffn_up(xb, w_a, w_g, conv_w, conv_b, tm, n_col):
    B, S, D = xb.shape
    F = w_a.shape[1]
    tm = min(tm, S)
    tn = F // n_col
    tiles_per_seq = S // tm
    x2 = xb.reshape(B * S, D)
    out = pl.pallas_call(
        functools.partial(_ffn_up_kernel, tiles_per_seq=tiles_per_seq),
        out_shape=jax.ShapeDtypeStruct((B * S, F), BF16),
        grid=(n_col, B * S // tm),
        in_specs=[pl.BlockSpec((tm, D), lambda n, i: (i, 0)),
                  pl.BlockSpec((D, tn), lambda n, i: (0, n)),
                  pl.BlockSpec((D, tn), lambda n, i: (0, n)),
                  pl.BlockSpec((CONV_WIDTH, tn), lambda n, i: (0, n)),
                  pl.BlockSpec((1, tn), lambda n, i: (0, n))],
        out_specs=pl.BlockSpec((tm, tn), lambda n, i: (i, n)),
        scratch_shapes=[pltpu.VMEM((8, tn), F32)],
        compiler_params=_params(2),
        name="ffn_up_conv",
    )(x2, w_a, w_g, conv_w, conv_b)
    return out.reshape(B, S, F)


def _ffn_down_kernel(x_ref, a_ref, w_ref, lg_ref, lb_ref, of_ref, ob_ref, *, alpha):
    h = _dot(a_ref[...], w_ref[...])
    y = _layer_norm(alpha * x_ref[...] + h, lg_ref[...], lb_ref[...])
    of_ref[...] = y
    ob_ref[...] = y.astype(BF16)


def _ffn_down(x, act, w_down, lg, lb, alpha, tm):
    B, S, D = x.shape
    F = act.shape[2]
    tm = min(tm, S)
    row = lambda w: pl.BlockSpec((None, tm, w), lambda b, i: (b, i, 0))
    full = lambda a: pl.BlockSpec(a.shape, lambda b, i: (0,) * a.ndim)
    return pl.pallas_call(
        functools.partial(_ffn_down_kernel, alpha=alpha),
        out_shape=(jax.ShapeDtypeStruct((B, S, D), F32), jax.ShapeDtypeStruct((B, S, D), BF16)),
        grid=(B, S // tm),
        in_specs=[row(D), row(F), full(w_down), full(lg), full(lb)],
        out_specs=(row(D), row(D)),
        compiler_params=_params(2),
        name="ffn_down_ln",
    )(x, act, w_down, lg, lb)


def _split_in_proj(w_in, b_in, d_model):
    o = np.cumsum([0, FOX_WIDTH, FOX_WIDTH, FOX_WIDTH, FOX_HEADS, SSM_WIDTH,
                   CA_WIDTH, CA_WIDTH, CA_WIDTH, 3 * d_model])
    fqkv, ff, su, cqkv, gates = (slice(o[0], o[3]), slice(o[3], o[4]), slice(o[4], o[5]),
                                 slice(o[5], o[8]), slice(o[8], o[9]))
    cols = lambda s: (w_in[:, s], b_in[s])
    cat = lambda a, b: (jnp.concatenate([a[0], b[0]], axis=1), jnp.concatenate([a[1], b[1]]))
    w_main, b_main = cat(cols(gates), cat(cols(fqkv), cols(cqkv)))
    w_su, b_su = cols(su)
    w_ff, b_ff = cols(ff)
    pad = LANES - FOX_HEADS
    w_ff = jnp.pad(w_ff, ((0, 0), (0, pad)))
    b_ff = jnp.pad(b_ff, (0, pad))
    prep = lambda w, b: (w.astype(BF16), b.astype(F32)[None, :])
    return prep(w_main, b_main), prep(w_su, b_su), prep(w_ff, b_ff)


def kernel(x, mem, w_in, b_in, ssm_lambda_re, ssm_lambda_im, ssm_log_dt, ssm_b_re, ssm_b_im, ssm_c_re, ssm_c_im, ssm_d, ca_rel_bias, w_fox_o, w_ssm_glu, w_ca_o, w_o, xa_wq, xa_wkv, xa_wo, ffn_w_up, ffn_conv_w, ffn_conv_b, ffn_w_down, ln_g, ln_b):
    B, S, D = x.shape
    depth = w_in.shape[0]
    d_ff = ffn_w_down.shape[1]
    alpha = float((2 * depth) ** 0.25)
    mem_b = mem.astype(BF16)
    xf = x.astype(F32)
    xb = x.astype(BF16)
    zero_bias = jnp.zeros((1, 2 * D), F32)
    for l in range(depth):
        (w_main, b_main), (w_su, b_su), (w_ff, b_ff) = _split_in_proj(w_in[l], b_in[l], D)
        z = _matmul(xb, w_main, b_main, BF16, tm=1024, tn=1024, name="in_proj")
        su = _matmul(xb, w_su, b_su, BF16, tm=1024, tn=512, time_major=True, name="in_proj_ssm")
        ff = _matmul(xb, w_ff, b_ff, F32, tm=1024, tn=LANES, name="in_proj_forget")
        cum = _forget_cumsum(ff)
        fox_base = 3 * D // LANES
        ya = _fox_attention(z, cum, fox_base)
        yc = _chunk_attention(z, _ca_table(ca_rel_bias[l], ATT_BLOCK), fox_base + 3 * FOX_WIDTH // LANES)
        bbr, bbi, cc, ar, ai = _s5_params(ssm_lambda_re[l], ssm_lambda_im[l], ssm_log_dt[l],
                                          ssm_b_re[l], ssm_b_im[l], ssm_c_re[l], ssm_c_im[l], B)
        yb = _s5_branch(su.reshape(S * B, SSM_WIDTH), bbr, bbi, cc, ar, ai,
                        ssm_d[l].astype(F32)[None, :], w_ssm_glu[l].astype(BF16), B)
        ln = lambda k: (ln_g[l, k].astype(F32)[None, :], ln_b[l, k].astype(F32)[None, :])
        xf, xb = _merge(xf, z, ya, yb.reshape(S, B * D), yc, w_fox_o[l].astype(BF16),
                        w_ca_o[l].astype(BF16), w_o[l].astype(BF16), *ln(0), alpha, tm=512)
        kv = _matmul(mem_b, xa_wkv[l].astype(BF16), zero_bias, BF16, tm=256, tn=1024, name="mem_kv")
        xf, xb = _cross_attention(xf, xb, kv, xa_wq[l].astype(BF16), xa_wo[l].astype(BF16),
                                  *ln(1), alpha, tm=512)
        w_up = ffn_w_up[l].astype(BF16)
        act = _ffn_up(xb, w_up[:, :d_ff], w_up[:, d_ff:], ffn_conv_w[l].astype(F32),
                      ffn_conv_b[l].astype(F32)[None, :], tm=512, n_col=2)
        xf, xb = _ffn_down(xf, act, ffn_w_down[l].astype(BF16), *ln(2), alpha, tm=512)
    return xf.astype(x.dtype)
```

```python
import functools
import math

import jax
import jax.numpy as jnp
import numpy as np
from jax import lax
from jax.experimental import pallas as pl
from jax.experimental.pallas import tpu as pltpu

F32 = jnp.float32
BF16 = jnp.bfloat16

CHUNK = 64
Q_BLOCK = 128
FOX_HEADS = 8
FOX_HEAD_DIM = 64
FOX_WIDTH = FOX_HEADS * FOX_HEAD_DIM
SSM_GROUP = 16
SSM_WIDTH = 512
SSM_GROUPS = SSM_WIDTH // SSM_GROUP
SSM_STATE = 64
CA_HEADS = 8
CA_HEAD_DIM = 64
CA_WIDTH = CA_HEADS * CA_HEAD_DIM
CA_LEFT_CHUNKS = 8
REL_MIN = -(CHUNK - 1)
REL_MAX = 4 * CHUNK
XA_HEADS = 4
CONV_WIDTH = 3
LN_EPS = 1e-5
NEG_INF = -1e30

LANES = 128
HEAD_PAIR = LANES // FOX_HEAD_DIM
ATT_BLOCK = 256
FOX_BLOCK = 512
CA_WINDOW_BLOCKS = 3
SSM_SLAB_GROUPS = LANES // SSM_GROUP
SSM_SLABS = SSM_GROUPS // SSM_SLAB_GROUPS
SSM_SLAB_STATE = SSM_SLAB_GROUPS * SSM_STATE
SSM_TIME_BLOCK = 128
MERGE_SUB_COLS = 256
FFN_SUB_COLS = 256
FFN_DOWN_GROUP = 4
FFN_HALO_ROWS = 8
VMEM_LIMIT = 56 * 1024 * 1024

_ARB = "arbitrary"


def _params(n_axes):
    return pltpu.CompilerParams(dimension_semantics=(_ARB,) * n_axes,
                                vmem_limit_bytes=VMEM_LIMIT)


def _layer_norm(v, g, b):
    mu = jnp.mean(v, axis=-1, keepdims=True)
    d = v - mu
    var = jnp.mean(d * d, axis=-1, keepdims=True)
    return d * lax.rsqrt(var + LN_EPS) * g + b


def _dot(a, b):
    return jnp.dot(a, b, preferred_element_type=F32)


def _dot_nt(a, b):
    return lax.dot_general(a, b, (((1,), (1,)), ((), ())), preferred_element_type=F32)


def _mm_kernel(x_ref, w_ref, b_ref, o_ref):
    o_ref[...] = (_dot(x_ref[...], w_ref[...]) + b_ref[...]).astype(o_ref.dtype)


def _matmul(x, w, bias, out_dtype, *, tm, tn, name="matmul"):
    B, S, K = x.shape
    N = w.shape[1]
    tm, tn = min(tm, S), min(tn, N)
    return pl.pallas_call(
        _mm_kernel,
        out_shape=jax.ShapeDtypeStruct((B, S, N), out_dtype),
        grid=(B, S // tm, N // tn),
        in_specs=[pl.BlockSpec((None, tm, K), lambda b, i, j: (b, i, 0)),
                  pl.BlockSpec((K, tn), lambda b, i, j: (0, j)),
                  pl.BlockSpec((1, tn), lambda b, i, j: (0, j))],
        out_specs=pl.BlockSpec((None, tm, tn), lambda b, i, j: (b, i, j)),
        compiler_params=_params(3),
        name=name,
    )(x, w, bias)


def _split3(c):
    hi = c.astype(BF16).astype(F32)
    r = c - hi
    mid = r.astype(BF16).astype(F32)
    lo = (r - mid).astype(BF16).astype(F32)
    return hi, mid, lo


def _cum_kernel(x_ref, w_ref, b_ref, o_ref, *, blk):
    S = x_ref.shape[0]
    row = lax.broadcasted_iota(jnp.int32, (blk, blk), 0)
    col = lax.broadcasted_iota(jnp.int32, (blk, blk), 1)
    tri = jnp.where(row >= col, 1.0, 0.0).astype(BF16)
    carry = jnp.zeros((1, LANES), F32)
    for i in range(S // blk):
        x = _dot(x_ref[i * blk:(i + 1) * blk, :], w_ref[...]) + b_ref[...]
        ls = jnp.minimum(x, 0.0) - jnp.log(1.0 + jnp.exp(-jnp.abs(x)))
        hi, mid, lo = _split3(ls)
        cs = (_dot(tri, hi.astype(BF16)) + _dot(tri, mid.astype(BF16))
              + _dot(tri, lo.astype(BF16))) + carry
        o_ref[i * blk:(i + 1) * blk, :] = cs
        carry = cs[blk - 1:blk, :]


def _forget_cumsum(xb, w_ff, b_ff):
    B, S, D = xb.shape
    L = w_ff.shape[1]
    return pl.pallas_call(
        functools.partial(_cum_kernel, blk=min(ATT_BLOCK, S)),
        out_shape=jax.ShapeDtypeStruct((B, S, L), F32),
        grid=(B,),
        in_specs=[pl.BlockSpec((None, S, D), lambda b: (b, 0, 0)),
                  pl.BlockSpec((D, L), lambda b: (0, 0)),
                  pl.BlockSpec((1, L), lambda b: (0, 0))],
        out_specs=pl.BlockSpec((None, S, L), lambda b: (b, 0, 0)),
        compiler_params=_params(1),
        name="forget_cumsum",
    )(xb, w_ff, b_ff)


def _head_lane_mask(j, shape):
    lane = lax.broadcasted_iota(jnp.int32, shape, len(shape) - 1)
    lo = j * FOX_HEAD_DIM
    return (lane >= lo) & (lane < lo + FOX_HEAD_DIM)


def _decay_lanes(cum_col, j, rows, query_side):
    a0 = (1 - j) * FOX_HEAD_DIM
    lane = lax.broadcasted_iota(jnp.int32, (rows, LANES), 1) - a0
    hi, mid, lo = _split3(cum_col)
    if query_side:
        vals = (hi, mid, lo, 1.0, 1.0, 1.0)
    else:
        vals = (1.0, 1.0, 1.0, -hi, -mid, -lo)
    out = jnp.zeros((rows, LANES), F32)
    for i, v in enumerate(vals):
        out = jnp.where(lane == i, v, out)
    return out


def _cum_column(cum_blk, h):
    lane = lax.broadcasted_iota(jnp.int32, cum_blk.shape, 1)
    return jnp.sum(jnp.where(lane == h, cum_blk, 0.0), axis=1, keepdims=True)


def _fox_kernel(q_ref, k_ref, v_ref, cum_ref, o_ref, kaug_ref, vaug_ref, s_ref, *, blk):
    p = pl.program_id(1)
    S = k_ref.shape[0]
    n_blk = S // blk

    for r in range(n_blk):
        rows = slice(r * blk, (r + 1) * blk)
        kf = k_ref[rows, :].astype(F32)
        vf = v_ref[rows, :].astype(F32)
        cum_blk = cum_ref[rows, :]
        for j in range(HEAD_PAIR):
            own = _head_lane_mask(j, kf.shape)
            col = _cum_column(cum_blk, HEAD_PAIR * p + j)
            aug = _decay_lanes(col, j, blk, query_side=False)
            kaug_ref[j, rows, :] = jnp.where(own, kf, aug).astype(BF16)
            vaug_ref[j, rows, :] = jnp.where(own, vf, 1.0).astype(BF16)

    for qi in range(n_blk):
        rows = slice(qi * blk, (qi + 1) * blk)
        qf = q_ref[rows, :].astype(F32) * (FOX_HEAD_DIM ** -0.5)
        cum_q = cum_ref[rows, :]
        slot = qi % 2
        outs = []
        for j in range(HEAD_PAIR):
            col = _cum_column(cum_q, HEAD_PAIR * p + j)
            aug = _decay_lanes(col, j, blk, query_side=True)
            qaug = jnp.where(_head_lane_mask(j, qf.shape), qf, aug).astype(BF16)
            ml = None
            for c in range(qi + 1):
                s = _dot_nt(qaug, kaug_ref[j, c * blk:(c + 1) * blk, :])
                if c == qi:
                    row = lax.broadcasted_iota(jnp.int32, (blk, blk), 0)
                    colk = lax.broadcasted_iota(jnp.int32, (blk, blk), 1)
                    s = jnp.where(row >= colk, s, NEG_INF)
                s_ref[slot, j, c] = s
                for g in range(blk // LANES):
                    sg = s[:, g * LANES:(g + 1) * LANES]
                    ml = sg if ml is None else jnp.maximum(ml, sg)
            m = jnp.max(ml, axis=1, keepdims=True)
            mb = jnp.concatenate([jnp.broadcast_to(m, (blk, LANES))] * (blk // LANES), axis=1)
            acc = None
            for c in range(qi + 1):
                pr = jnp.exp(s_ref[slot, j, c] - mb)
                d = _dot(pr.astype(BF16), vaug_ref[j, c * blk:(c + 1) * blk, :])
                acc = d if acc is None else acc + d
            outs.append(acc / pltpu.roll(acc, FOX_HEAD_DIM, 1))
        o_ref[rows, :] = jnp.where(_head_lane_mask(0, outs[0].shape), outs[0], outs[1]).astype(o_ref.dtype)


def _fox_attention(z, cum, base):
    B, S, _ = z.shape
    blk = min(FOX_BLOCK, S)
    n_pair = FOX_WIDTH // LANES
    return pl.pallas_call(
        functools.partial(_fox_kernel, blk=blk),
        out_shape=jax.ShapeDtypeStruct((B, S, FOX_WIDTH), BF16),
        grid=(B, n_pair),
        in_specs=[pl.BlockSpec((None, S, LANES), lambda b, p: (b, 0, base + p)),
                  pl.BlockSpec((None, S, LANES), lambda b, p: (b, 0, base + n_pair + p)),
                  pl.BlockSpec((None, S, LANES), lambda b, p: (b, 0, base + 2 * n_pair + p)),
                  pl.BlockSpec((None, S, LANES), lambda b, p: (b, 0, 0))],
        out_specs=pl.BlockSpec((None, S, LANES), lambda b, p: (b, 0, p)),
        scratch_shapes=[pltpu.VMEM((HEAD_PAIR, S, LANES), BF16),
                        pltpu.VMEM((HEAD_PAIR, S, LANES), BF16),
                        pltpu.VMEM((2, HEAD_PAIR, S // blk, blk, blk), F32)],
        compiler_params=_params(2),
        name="fox_attention",
    )(z, z, z, cum)


def _ca_kernel(q_ref, k_ref, v_ref, u_ref, o_ref, vaug_ref, *, blk):
    S = q_ref.shape[0]
    left = CA_WINDOW_BLOCKS - 1
    for r in range(S // blk):
        rows = slice(r * blk, (r + 1) * blk)
        vf = v_ref[rows, :].astype(F32)
        for hh in range(HEAD_PAIR):
            vaug_ref[hh, rows, :] = jnp.where(_head_lane_mask(hh, vf.shape), vf, 1.0).astype(BF16)

    for j in range(S // blk):
        qf = q_ref[j * blk:(j + 1) * blk, :].astype(F32) * (CA_HEAD_DIM ** -0.5)
        kblocks = [(j - left + i, i) for i in range(CA_WINDOW_BLOCKS) if j - left + i >= 0]
        outs = []
        for hh in range(HEAD_PAIR):
            qm = jnp.where(_head_lane_mask(hh, qf.shape), qf, 0.0).astype(BF16)
            s = [_dot_nt(qm, k_ref[kb * blk:(kb + 1) * blk, :]) + u_ref[hh, i] for kb, i in kblocks]
            ml = s[0][:, :LANES]
            for sb in s:
                for g in range(blk // LANES):
                    ml = jnp.maximum(ml, sb[:, g * LANES:(g + 1) * LANES])
            m = jnp.max(ml, axis=1, keepdims=True)
            acc = jnp.zeros((blk, LANES), F32)
            for sb, (kb, _) in zip(s, kblocks):
                pr = jnp.exp(sb - m)
                acc = acc + _dot(pr.astype(BF16), vaug_ref[hh, kb * blk:(kb + 1) * blk, :])
            outs.append(acc / pltpu.roll(acc, CA_HEAD_DIM, 1))
        o_ref[j * blk:(j + 1) * blk, :] = jnp.where(
            _head_lane_mask(0, outs[0].shape), outs[0], outs[1]).astype(o_ref.dtype)


def _ca_table(rel_bias, blk):
    H = rel_bias.shape[0]
    n_m = CA_WINDOW_BLOCKS * blk
    shift = (CA_WINDOW_BLOCKS - 1) * blk
    qi = np.arange(blk)[:, None]
    k_rel = np.arange(n_m)[None, :] - shift
    dchunk = qi // CHUNK - np.floor_divide(k_rel, CHUNK)
    valid = (dchunk >= 0) & (dchunk <= CA_LEFT_CHUNKS)
    period = n_m + blk
    x = np.arange(period)
    x = np.where(x >= n_m, x - period, x)
    vec = rel_bias.astype(F32)[:, np.clip(shift - x, REL_MIN, REL_MAX) - REL_MIN]
    toep = jnp.tile(vec, (1, blk))[:, :blk * (period - 1)].reshape(H, blk, period - 1)[:, :, :n_m]
    table = jnp.where(jnp.asarray(valid)[None], toep, NEG_INF)
    return table.reshape(H, blk, CA_WINDOW_BLOCKS, blk).transpose(0, 2, 1, 3)


def _chunk_attention(z, table, base):
    B, S, _ = z.shape
    blk = ATT_BLOCK
    n_pair = CA_WIDTH // LANES
    return pl.pallas_call(
        functools.partial(_ca_kernel, blk=blk),
        out_shape=jax.ShapeDtypeStruct((B, S, CA_WIDTH), BF16),
        grid=(n_pair, B),
        in_specs=[pl.BlockSpec((None, S, LANES), lambda p, b: (b, 0, base + p)),
                  pl.BlockSpec((None, S, LANES), lambda p, b: (b, 0, base + n_pair + p)),
                  pl.BlockSpec((None, S, LANES), lambda p, b: (b, 0, base + 2 * n_pair + p)),
                  pl.BlockSpec((HEAD_PAIR, CA_WINDOW_BLOCKS, blk, blk), lambda p, b: (p, 0, 0, 0))],
        out_specs=pl.BlockSpec((None, S, LANES), lambda p, b: (b, 0, p)),
        scratch_shapes=[pltpu.VMEM((HEAD_PAIR, S, LANES), BF16)],
        compiler_params=_params(2),
        name="chunk_attention",
    )(z, z, z, table)


def _s5_kernel(x_ref, wsu_ref, bsu_ref, bbr_ref, bbi_ref, cc_ref, ar_ref, ai_ref, d_ref, wg_ref, o_ref,
               ut_ref, xr_ref, xi_ref, yt_ref, hr_ref, hi_ref, *, steps, batch):
    @pl.when(pl.program_id(0) == 0)
    def _init():
        hr_ref[...] = jnp.zeros_like(hr_ref)
        hi_ref[...] = jnp.zeros_like(hi_ref)

    for b in range(batch):
        su = _dot(x_ref[b], wsu_ref[...]) + bsu_ref[...]
        for i in range(SSM_SLABS):
            ut_ref[i, pl.ds(b, steps, stride=batch), :] = su[:, i * LANES:(i + 1) * LANES]

    for i in range(SSM_SLABS):
        ui = ut_ref[i].astype(BF16)
        cols = slice(i * SSM_SLAB_STATE, (i + 1) * SSM_SLAB_STATE)
        xr_ref[:, cols] = _dot(ui, bbr_ref[i])
        xi_ref[:, cols] = _dot(ui, bbi_ref[i])

    for i in range(SSM_SLABS):
        cols = slice(i * SSM_SLAB_STATE, (i + 1) * SSM_SLAB_STATE)
        ar = ar_ref[:, cols]
        ai = ai_ref[:, cols]

        def body(t, carry, cols=cols, ar=ar, ai=ai):
            hr, hi = carry
            r0 = pl.multiple_of(t * batch, batch)
            nhr = ar * hr - ai * hi + xr_ref[pl.ds(r0, batch), cols]
            nhi = ar * hi + ai * hr + xi_ref[pl.ds(r0, batch), cols]
            xr_ref[pl.ds(r0, batch), cols] = nhr
            xi_ref[pl.ds(r0, batch), cols] = nhi
            return nhr, nhi

        hr, hi = lax.fori_loop(0, steps, body, (hr_ref[:, cols], hi_ref[:, cols]), unroll=8)
        hr_ref[:, cols] = hr
        hi_ref[:, cols] = hi

    ys = []
    for i in range(SSM_SLABS):
        cols = slice(i * SSM_SLAB_STATE, (i + 1) * SSM_SLAB_STATE)
        h = jnp.concatenate([xr_ref[:, cols].astype(BF16), xi_ref[:, cols].astype(BF16)], axis=1)
        ys.append(_dot(h, cc_ref[i]) + d_ref[:, i * LANES:(i + 1) * LANES] * ut_ref[i])
    z = _dot(jax.nn.gelu(jnp.concatenate(ys, axis=1)).astype(BF16), wg_ref[...])
    half = z.shape[1] // 2
    out = z[:, :half] * jax.nn.sigmoid(z[:, half:])
    for k in range(half // LANES):
        yt_ref[k] = out[:, k * LANES:(k + 1) * LANES]
    for b in range(batch):
        for k in range(half // LANES):
            o_ref[b, :, k * LANES:(k + 1) * LANES] = (
                yt_ref[k, pl.ds(b, steps, stride=batch), :].astype(o_ref.dtype))


def _s5_branch(xb, w_su, b_su, bbr, bbi, cc, ar, ai, d, w_glu):
    B, S, D = xb.shape
    steps = min(SSM_TIME_BLOCK, S)
    tr = steps * B
    n_state = SSM_GROUPS * SSM_STATE
    d_out = w_glu.shape[1] // 2
    full = lambda a: pl.BlockSpec(a.shape, lambda t: (0,) * a.ndim)
    return pl.pallas_call(
        functools.partial(_s5_kernel, steps=steps, batch=B),
        out_shape=jax.ShapeDtypeStruct((B, S, d_out), BF16),
        grid=(S // steps,),
        in_specs=[pl.BlockSpec((B, steps, D), lambda t: (0, t, 0)),
                  full(w_su), full(b_su), full(bbr), full(bbi), full(cc), full(ar), full(ai),
                  full(d), full(w_glu)],
        out_specs=pl.BlockSpec((B, steps, d_out), lambda t: (0, t, 0)),
        scratch_shapes=[pltpu.VMEM((SSM_SLABS, tr, LANES), F32),
                        pltpu.VMEM((tr, n_state), F32), pltpu.VMEM((tr, n_state), F32),
                        pltpu.VMEM((d_out // LANES, tr, LANES), F32),
                        pltpu.VMEM((B, n_state), F32), pltpu.VMEM((B, n_state), F32)],
        compiler_params=_params(1),
        name="s5_branch",
    )(xb, w_su, b_su, bbr, bbi, cc, ar, ai, d, w_glu)


def _s5_params(lam_re, lam_im, log_dt, b_re, b_im, c_re, c_im, batch):
    lr = jnp.minimum(lam_re.astype(F32), -1e-4)
    li = lam_im.astype(F32)
    dt = jnp.exp(log_dt.astype(F32))[:, None]
    mag = jnp.exp(lr * dt)
    ar = mag * jnp.cos(li * dt)
    ai = mag * jnp.sin(li * dt)
    den = lr * lr + li * li
    gr = ((ar - 1.0) * lr + ai * li) / den
    gi = (ai * lr - (ar - 1.0) * li) / den
    br, bi = b_re.astype(F32), b_im.astype(F32)
    bbr = gr[..., None] * br - gi[..., None] * bi
    bbi = gr[..., None] * bi + gi[..., None] * br
    eye = jnp.eye(SSM_SLAB_GROUPS, dtype=F32)

    def in_map(bb):
        bb = bb.reshape(SSM_SLABS, SSM_SLAB_GROUPS, SSM_STATE, SSM_GROUP)
        return jnp.einsum('sgpc,gh->sgchp', bb, eye).reshape(SSM_SLABS, LANES, SSM_SLAB_STATE).astype(BF16)

    def out_map(c):
        c = c.astype(F32).reshape(SSM_SLABS, SSM_SLAB_GROUPS, SSM_GROUP, SSM_STATE)
        return jnp.einsum('sgcp,gh->sgphc', c, eye).reshape(SSM_SLABS, SSM_SLAB_STATE, LANES)

    cc = jnp.concatenate([out_map(c_re), -out_map(c_im)], axis=1).astype(BF16)
    n_state = SSM_GROUPS * SSM_STATE
    ar_b = jnp.broadcast_to(ar.reshape(1, n_state), (batch, n_state))
    ai_b = jnp.broadcast_to(ai.reshape(1, n_state), (batch, n_state))
    return in_map(bbr), in_map(bbi), cc, ar_b, ai_b


def _merge_kernel(x_ref, xb_ref, ya_ref, yb_ref, yc_ref, wg_ref, bg_ref, wa_ref, wc_ref, wo_ref,
                  lg_ref, lb_ref, of_ref, ob_ref, m_ref, *, alpha):
    d = x_ref.shape[1]
    xb = xb_ref[...]
    ya = ya_ref[...]
    yc = yc_ref[...]
    for c in range(d // MERGE_SUB_COLS):
        cols = slice(c * MERGE_SUB_COLS, (c + 1) * MERGE_SUB_COLS)
        def gate(k, cols=cols):
            gcols = slice(k * d + cols.start, k * d + cols.stop)
            return jax.nn.sigmoid(_dot(xb, wg_ref[:, gcols]) + bg_ref[:, gcols])

        merged = (gate(0) * _dot(ya, wa_ref[:, cols])
                  + gate(1) * yb_ref[:, cols].astype(F32)
                  + gate(2) * _dot(yc, wc_ref[:, cols]))
        m_ref[:, cols] = merged.astype(BF16)
    h = _dot(m_ref[...], wo_ref[...])
    y = _layer_norm(alpha * x_ref[...] + h, lg_ref[...], lb_ref[...])
    of_ref[...] = y
    ob_ref[...] = y.astype(BF16)


def _merge(x, xb, ya, yb, yc, w_gate, b_gate, w_fox_o, w_ca_o, w_o, lg, lb, alpha, tm):
    B, S, D = x.shape
    tm = min(tm, S)
    row = lambda w: pl.BlockSpec((None, tm, w), lambda b, i: (b, i, 0))
    full = lambda a: pl.BlockSpec(a.shape, lambda b, i: (0,) * a.ndim)
    return pl.pallas_call(
        functools.partial(_merge_kernel, alpha=alpha),
        out_shape=(jax.ShapeDtypeStruct((B, S, D), F32), jax.ShapeDtypeStruct((B, S, D), BF16)),
        grid=(B, S // tm),
        in_specs=[row(D), row(D), row(ya.shape[2]), row(D), row(yc.shape[2]),
                  full(w_gate), full(b_gate), full(w_fox_o), full(w_ca_o), full(w_o), full(lg), full(lb)],
        out_specs=(row(D), row(D)),
        scratch_shapes=[pltpu.VMEM((tm, D), BF16)],
        compiler_params=_params(2),
        name="merge_proj_ln",
    )(x, xb, ya, yb, yc, w_gate, b_gate, w_fox_o, w_ca_o, w_o, lg, lb)


def _xattn_kernel(x_ref, xb_ref, kv_ref, wq_ref, wo_ref, lg_ref, lb_ref, of_ref, ob_ref, *, alpha):
    d = x_ref.shape[1]
    dh = d // XA_HEADS
    q = (_dot(xb_ref[...], wq_ref[...]) * (dh ** -0.5)).astype(BF16)
    outs = []
    for h in range(XA_HEADS):
        s = _dot_nt(q[:, h * dh:(h + 1) * dh], kv_ref[:, h * dh:(h + 1) * dh])
        m = jnp.max(s, axis=-1, keepdims=True)
        pr = jnp.exp(s - m)
        l = jnp.sum(pr, axis=-1, keepdims=True)
        outs.append((_dot(pr.astype(BF16), kv_ref[:, d + h * dh:d + (h + 1) * dh]) / l).astype(BF16))
    hout = _dot(jnp.concatenate(outs, axis=1), wo_ref[...])
    y = _layer_norm(alpha * x_ref[...] + hout, lg_ref[...], lb_ref[...])
    of_ref[...] = y
    ob_ref[...] = y.astype(BF16)


def _cross_attention(x, xb, kv, wq, wo, lg, lb, alpha, tm):
    B, S, D = x.shape
    M = kv.shape[1]
    tm = min(tm, S)
    row = pl.BlockSpec((None, tm, D), lambda b, i: (b, i, 0))
    full = lambda a: pl.BlockSpec(a.shape, lambda b, i: (0,) * a.ndim)
    return pl.pallas_call(
        functools.partial(_xattn_kernel, alpha=alpha),
        out_shape=(jax.ShapeDtypeStruct((B, S, D), F32), jax.ShapeDtypeStruct((B, S, D), BF16)),
        grid=(B, S // tm),
        in_specs=[row, row, pl.BlockSpec((None, M, 2 * D), lambda b, i: (b, 0, 0)),
                  full(wq), full(wo), full(lg), full(lb)],
        out_specs=(row, row),
        compiler_params=_params(2),
        name="cross_attention_ln",
    )(x, xb, kv, wq, wo, lg, lb)


def _ffn_kernel(x_ref, xb_ref, wup_ref, cw_ref, cb_ref, wdn_ref, lg_ref, lb_ref, of_ref, ob_ref,
                abuf_ref, act_ref, *, alpha, d_ff):
    tm = x_ref.shape[0]
    pad = FFN_HALO_ROWS

    @pl.when(pl.program_id(1) == 0)
    def _seq_start():
        abuf_ref[0:pad, :] = jnp.zeros((pad, d_ff), F32)

    xb = xb_ref[...]
    h = jnp.zeros(x_ref.shape, F32)
    n_sub = d_ff // FFN_SUB_COLS
    for j in range(n_sub):
        cols = slice(j * FFN_SUB_COLS, (j + 1) * FFN_SUB_COLS)
        gcols = slice(d_ff + j * FFN_SUB_COLS, d_ff + (j + 1) * FFN_SUB_COLS)
        a = _dot(xb, wup_ref[:, cols])
        g = _dot(xb, wup_ref[:, gcols])
        abuf_ref[pad:pad + tm, cols] = a
        a1 = abuf_ref[pad - 1:pad - 1 + tm, cols]
        a2 = abuf_ref[pad - 2:pad - 2 + tm, cols]
        conv = (cw_ref[0:1, cols] * a2 + cw_ref[1:2, cols] * a1 + cw_ref[2:3, cols] * a
                + cb_ref[:, cols])
        act_ref[:, cols] = (jax.nn.gelu(conv) * g).astype(BF16)
        if (j + 1) % FFN_DOWN_GROUP == 0 or j == n_sub - 1:
            lo = (j // FFN_DOWN_GROUP) * FFN_DOWN_GROUP * FFN_SUB_COLS
            rows = slice(lo, (j + 1) * FFN_SUB_COLS)
            h = h + _dot(act_ref[:, rows], wdn_ref[rows, :])
    abuf_ref[0:pad, :] = abuf_ref[tm:tm + pad, :]
    y = _layer_norm(alpha * x_ref[...] + h, lg_ref[...], lb_ref[...])
    of_ref[...] = y
    ob_ref[...] = y.astype(BF16)


def _conv_ffn(x, xb, w_up, conv_w, conv_b, w_down, lg, lb, alpha, tm):
    B, S, D = x.shape
    d_ff = w_down.shape[0]
    tm = min(tm, S)
    row = pl.BlockSpec((None, tm, D), lambda b, i: (b, i, 0))
    full = lambda a: pl.BlockSpec(a.shape, lambda b, i: (0,) * a.ndim)
    once = lambda a: pl.BlockSpec(a.shape, lambda b, i: (0,) * a.ndim, pipeline_mode=pl.Buffered(1))
    return pl.pallas_call(
        functools.partial(_ffn_kernel, alpha=alpha, d_ff=d_ff),
        out_shape=(jax.ShapeDtypeStruct((B, S, D), F32), jax.ShapeDtypeStruct((B, S, D), BF16)),
        grid=(B, S // tm),
        in_specs=[row, row, once(w_up), full(conv_w), full(conv_b), once(w_down), full(lg), full(lb)],
        out_specs=(row, row),
        scratch_shapes=[pltpu.VMEM((tm + FFN_HALO_ROWS, d_ff), F32), pltpu.VMEM((tm, d_ff), BF16)],
        compiler_params=_params(2),
        name="conv_ffn_ln",
    )(x, xb, w_up, conv_w, conv_b, w_down, lg, lb)


def _split_in_proj(w_in, b_in, d_model):
    o = np.cumsum([0, FOX_WIDTH, FOX_WIDTH, FOX_WIDTH, FOX_HEADS, SSM_WIDTH,
                   CA_WIDTH, CA_WIDTH, CA_WIDTH, 3 * d_model])
    fqkv, ff, su, cqkv, gates = (slice(o[0], o[3]), slice(o[3], o[4]), slice(o[4], o[5]),
                                 slice(o[5], o[8]), slice(o[8], o[9]))
    cols = lambda s: (w_in[:, s], b_in[s])
    w_qkv = jnp.concatenate([w_in[:, fqkv], w_in[:, cqkv]], axis=1)
    b_qkv = jnp.concatenate([b_in[fqkv], b_in[cqkv]])
    w_ff, b_ff = cols(ff)
    pad = LANES - FOX_HEADS
    w_ff = jnp.pad(w_ff, ((0, 0), (0, pad)))
    b_ff = jnp.pad(b_ff, (0, pad))
    prep = lambda w, b: (w.astype(BF16), b.astype(F32)[None, :])
    return prep(w_qkv, b_qkv), prep(*cols(gates)), prep(*cols(su)), prep(w_ff, b_ff)


def kernel(x, mem, w_in, b_in, ssm_lambda_re, ssm_lambda_im, ssm_log_dt, ssm_b_re, ssm_b_im, ssm_c_re, ssm_c_im, ssm_d, ca_rel_bias, w_fox_o, w_ssm_glu, w_ca_o, w_o, xa_wq, xa_wkv, xa_wo, ffn_w_up, ffn_conv_w, ffn_conv_b, ffn_w_down, ln_g, ln_b):
    B, S, D = x.shape
    depth = w_in.shape[0]
    d_ff = ffn_w_down.shape[1]
    alpha = float((2 * depth) ** 0.25)
    mem_b = mem.astype(BF16)
    xf = x.astype(F32)
    xb = x.astype(BF16)
    zero_bias = jnp.zeros((1, 2 * D), F32)
    for l in range(depth):
        (w_qkv, b_qkv), (w_gate, b_gate), (w_su, b_su), (w_ff, b_ff) = _split_in_proj(w_in[l], b_in[l], D)
        z = _matmul(xb, w_qkv, b_qkv, BF16, tm=1024, tn=1024, name="in_proj")
        cum = _forget_cumsum(xb, w_ff, b_ff)
        ya = _fox_attention(z, cum, 0)
        yc = _chunk_attention(z, _ca_table(ca_rel_bias[l], ATT_BLOCK), 3 * FOX_WIDTH // LANES)
        bbr, bbi, cc, ar, ai = _s5_params(ssm_lambda_re[l], ssm_lambda_im[l], ssm_log_dt[l],
                                          ssm_b_re[l], ssm_b_im[l], ssm_c_re[l], ssm_c_im[l], B)
        yb = _s5_branch(xb, w_su, b_su, bbr, bbi, cc, ar, ai,
                        ssm_d[l].astype(F32)[None, :], w_ssm_glu[l].astype(BF16))
        ln = lambda k: (ln_g[l, k].astype(F32)[None, :], ln_b[l, k].astype(F32)[None, :])
        xf, xb = _merge(xf, xb, ya, yb, yc, w_gate, b_gate, w_fox_o[l].astype(BF16),
                        w_ca_o[l].astype(BF16), w_o[l].astype(BF16), *ln(0), alpha, tm=512)
        kv = _matmul(mem_b, xa_wkv[l].astype(BF16), zero_bias, BF16, tm=256, tn=1024, name="mem_kv")
        xf, xb = _cross_attention(xf, xb, kv, xa_wq[l].astype(BF16), xa_wo[l].astype(BF16),
                                  *ln(1), alpha, tm=512)
        xf, xb = _conv_ffn(xf, xb, ffn_w_up[l].astype(BF16), ffn_conv_w[l].astype(F32),
                           ffn_conv_b[l].astype(F32)[None, :], ffn_w_down[l].astype(BF16),
                           *ln(2), alpha, tm=512)
    return xf.astype(x.dtype)
```

```python
import functools
import math

import jax
import jax.numpy as jnp
import numpy as np
from jax import lax
from jax.experimental import pallas as pl
from jax.experimental.pallas import tpu as pltpu

F32 = jnp.float32
BF16 = jnp.bfloat16

CHUNK = 64
Q_BLOCK = 128
FOX_HEADS = 8
FOX_HEAD_DIM = 64
FOX_WIDTH = FOX_HEADS * FOX_HEAD_DIM
SSM_GROUP = 16
SSM_WIDTH = 512
SSM_GROUPS = SSM_WIDTH // SSM_GROUP
SSM_STATE = 64
CA_HEADS = 8
CA_HEAD_DIM = 64
CA_WIDTH = CA_HEADS * CA_HEAD_DIM
CA_LEFT_CHUNKS = 8
REL_MIN = -(CHUNK - 1)
REL_MAX = 4 * CHUNK
XA_HEADS = 4
CONV_WIDTH = 3
LN_EPS = 1e-5
NEG_INF = -1e30

LOG2E = math.log2(math.e)
DECAY_LANES = 6
DECAY_STRIDE = 8
LANES = 128
HEAD_PAIR = LANES // FOX_HEAD_DIM
ATT_BLOCK = 256
FOX_BLOCK = 512
CA_WINDOW_BLOCKS = 3
SSM_SLAB_GROUPS = LANES // SSM_GROUP
SSM_SLABS = SSM_GROUPS // SSM_SLAB_GROUPS
SSM_SLAB_STATE = SSM_SLAB_GROUPS * SSM_STATE
SSM_TIME_BLOCK = 128
MATMUL_SUB_COLS = 512
MERGE_SUB_COLS = 256
FFN_SUB_COLS = 256
FFN_DOWN_GROUP = 4
FFN_HALO_ROWS = 8
VMEM_LIMIT = 56 * 1024 * 1024

_ARB = "arbitrary"


def _params(n_axes):
    return pltpu.CompilerParams(dimension_semantics=(_ARB,) * n_axes,
                                vmem_limit_bytes=VMEM_LIMIT)


def _layer_norm(v, g, b):
    mu = jnp.mean(v, axis=-1, keepdims=True)
    d = v - mu
    var = jnp.mean(d * d, axis=-1, keepdims=True)
    return d * lax.rsqrt(var + LN_EPS) * g + b


def _dot(a, b):
    return jnp.dot(a, b, preferred_element_type=F32)


def _dot_nt(a, b):
    return lax.dot_general(a, b, (((1,), (1,)), ((), ())), preferred_element_type=F32)


def _mm_kernel(x_ref, w_ref, b_ref, o_ref):
    x = x_ref[...]
    n = w_ref.shape[1]
    sub = min(MATMUL_SUB_COLS, n)
    for c in range(n // sub):
        cols = slice(c * sub, (c + 1) * sub)
        o_ref[:, cols] = (_dot(x, w_ref[:, cols]) + b_ref[:, cols]).astype(o_ref.dtype)


def _matmul(x, w, bias, out_dtype, *, tm, name="matmul"):
    B, S, K = x.shape
    N = w.shape[1]
    tm = min(tm, S)
    return pl.pallas_call(
        _mm_kernel,
        out_shape=jax.ShapeDtypeStruct((B, S, N), out_dtype),
        grid=(B, S // tm),
        in_specs=[pl.BlockSpec((None, tm, K), lambda b, i: (b, i, 0)),
                  pl.BlockSpec((K, N), lambda b, i: (0, 0)),
                  pl.BlockSpec((1, N), lambda b, i: (0, 0))],
        out_specs=pl.BlockSpec((None, tm, N), lambda b, i: (b, i, 0)),
        compiler_params=_params(2),
        name=name,
    )(x, w, bias)


def _split3(c):
    hi = c.astype(BF16).astype(F32)
    r = c - hi
    mid = r.astype(BF16).astype(F32)
    lo = (r - mid).astype(BF16).astype(F32)
    return hi, mid, lo


def _decay_scatter(query_side):
    r = lax.broadcasted_iota(jnp.int32, (3 * LANES, LANES), 0)
    c = lax.broadcasted_iota(jnp.int32, (3 * LANES, LANES), 1)
    piece, h = r // LANES, r % LANES
    dst = DECAY_STRIDE * h + piece + (0 if query_side else 3)
    mat = jnp.where((h < FOX_HEADS) & (c == dst), 1.0 if query_side else -1.0, 0.0).astype(BF16)
    lane = lax.broadcasted_iota(jnp.int32, (1, LANES), 1)
    k = lane % DECAY_STRIDE
    ones_lo = 3 if query_side else 0
    ones = jnp.where((lane < DECAY_STRIDE * FOX_HEADS) & (k >= ones_lo) & (k < ones_lo + 3), 1.0, 0.0)
    return mat, ones


def _cum_kernel(x_ref, w_ref, b_ref, qf_ref, kf_ref, *, blk):
    S = x_ref.shape[0]
    row = lax.broadcasted_iota(jnp.int32, (blk, blk), 0)
    col = lax.broadcasted_iota(jnp.int32, (blk, blk), 1)
    tri = jnp.where(row >= col, 1.0, 0.0).astype(BF16)
    q_mat, q_ones = _decay_scatter(True)
    k_mat, k_ones = _decay_scatter(False)
    carry = jnp.zeros((1, LANES), F32)
    for i in range(S // blk):
        x = _dot(x_ref[i * blk:(i + 1) * blk, :], w_ref[...]) + b_ref[...]
        ls = jnp.minimum(x, 0.0) - jnp.log(1.0 + jnp.exp(-jnp.abs(x)))
        hi, mid, lo = _split3(ls)
        cs = (_dot(tri, hi.astype(BF16)) + _dot(tri, mid.astype(BF16))
              + _dot(tri, lo.astype(BF16))) + carry
        carry = cs[blk - 1:blk, :]
        pieces = jnp.concatenate([v.astype(BF16) for v in _split3(cs * LOG2E)], axis=1)
        qf_ref[i * blk:(i + 1) * blk, :] = _dot(pieces, q_mat) + q_ones
        kf_ref[i * blk:(i + 1) * blk, :] = _dot(pieces, k_mat) + k_ones


def _forget_decay_features(xb, w_ff, b_ff):
    B, S, D = xb.shape
    L = w_ff.shape[1]
    out = jax.ShapeDtypeStruct((B, S, L), F32)
    spec = pl.BlockSpec((None, S, L), lambda b: (b, 0, 0))
    return pl.pallas_call(
        functools.partial(_cum_kernel, blk=min(ATT_BLOCK, S)),
        out_shape=(out, out),
        grid=(B,),
        in_specs=[pl.BlockSpec((None, S, D), lambda b: (b, 0, 0)),
                  pl.BlockSpec((D, L), lambda b: (0, 0)),
                  pl.BlockSpec((1, L), lambda b: (0, 0))],
        out_specs=(spec, spec),
        compiler_params=_params(1),
        name="forget_cumsum",
    )(xb, w_ff, b_ff)


def _head_lane_mask(j, shape):
    lane = lax.broadcasted_iota(jnp.int32, shape, len(shape) - 1)
    lo = j * FOX_HEAD_DIM
    return (lane >= lo) & (lane < lo + FOX_HEAD_DIM)


def _with_decay_lanes(x, feat, j, p):
    a0 = (1 - j) * FOX_HEAD_DIM
    shift = a0 - DECAY_STRIDE * j + (LANES if j else 0) - DECAY_STRIDE * HEAD_PAIR * p
    lane = lax.broadcasted_iota(jnp.int32, x.shape, 1)
    window = (lane >= a0) & (lane < a0 + DECAY_LANES)
    aug = jnp.where(window, pltpu.roll(feat, shift, 1), 0.0)
    return jnp.where(_head_lane_mask(j, x.shape), x, aug).astype(BF16)


def _fox_kernel(q_ref, k_ref, v_ref, qf_ref, kf_ref, o_ref, kaug_ref, vaug_ref, s_ref, mb_ref, *, blk):
    p = pl.program_id(1)
    S = k_ref.shape[0]
    n_blk = S // blk
    tri = lambda qi, c: qi * (qi + 1) // 2 + c

    for r in range(n_blk):
        rows = slice(r * blk, (r + 1) * blk)
        kf = k_ref[rows, :].astype(F32)
        vf = v_ref[rows, :].astype(F32)
        feat = kf_ref[rows, :]
        for j in range(HEAD_PAIR):
            kaug_ref[j, rows, :] = _with_decay_lanes(kf, feat, j, p)
            vaug_ref[j, rows, :] = jnp.where(_head_lane_mask(j, vf.shape), vf, 1.0).astype(BF16)

    for qi in range(n_blk):
        rows = slice(qi * blk, (qi + 1) * blk)
        qf = q_ref[rows, :].astype(F32) * (FOX_HEAD_DIM ** -0.5 * LOG2E)
        feat = qf_ref[rows, :]
        for j in range(HEAD_PAIR):
            qaug = _with_decay_lanes(qf, feat, j, p)
            ml = None
            for c in range(qi + 1):
                s = _dot_nt(qaug, kaug_ref[j, c * blk:(c + 1) * blk, :])
                if c == qi:
                    row = lax.broadcasted_iota(jnp.int32, (blk, blk), 0)
                    colk = lax.broadcasted_iota(jnp.int32, (blk, blk), 1)
                    s = jnp.where(row >= colk, s, NEG_INF)
                s_ref[tri(qi, c), j] = s
                for g in range(blk // LANES):
                    sg = s[:, g * LANES:(g + 1) * LANES]
                    ml = sg if ml is None else jnp.maximum(ml, sg)
            mb_ref[qi, j] = jnp.broadcast_to(jnp.max(ml, axis=1, keepdims=True), (blk, LANES))

    for qi in range(n_blk):
        outs = []
        for j in range(HEAD_PAIR):
            mb = jnp.concatenate([mb_ref[qi, j]] * (blk // LANES), axis=1)
            acc = None
            for c in range(qi + 1):
                pr = jnp.exp2((s_ref[tri(qi, c), j] - mb).astype(BF16))
                d = _dot(pr, vaug_ref[j, c * blk:(c + 1) * blk, :])
                acc = d if acc is None else acc + d
            outs.append(acc / pltpu.roll(acc, FOX_HEAD_DIM, 1))
        o_ref[qi * blk:(qi + 1) * blk, :] = jnp.where(
            _head_lane_mask(0, outs[0].shape), outs[0], outs[1]).astype(o_ref.dtype)


def _fox_attention(z, qfeat, kfeat, base):
    B, S, _ = z.shape
    blk = min(FOX_BLOCK, S)
    n_q = S // blk
    n_pair = FOX_WIDTH // LANES
    return pl.pallas_call(
        functools.partial(_fox_kernel, blk=blk),
        out_shape=jax.ShapeDtypeStruct((B, S, FOX_WIDTH), BF16),
        grid=(B, n_pair),
        in_specs=[pl.BlockSpec((None, S, LANES), lambda b, p: (b, 0, base + p)),
                  pl.BlockSpec((None, S, LANES), lambda b, p: (b, 0, base + n_pair + p)),
                  pl.BlockSpec((None, S, LANES), lambda b, p: (b, 0, base + 2 * n_pair + p)),
                  pl.BlockSpec((None, S, LANES), lambda b, p: (b, 0, 0)),
                  pl.BlockSpec((None, S, LANES), lambda b, p: (b, 0, 0))],
        out_specs=pl.BlockSpec((None, S, LANES), lambda b, p: (b, 0, p)),
        scratch_shapes=[pltpu.VMEM((HEAD_PAIR, S, LANES), BF16),
                        pltpu.VMEM((HEAD_PAIR, S, LANES), BF16),
                        pltpu.VMEM((n_q * (n_q + 1) // 2, HEAD_PAIR, blk, blk), F32),
                        pltpu.VMEM((n_q, HEAD_PAIR, blk, LANES), F32)],
        compiler_params=_params(2),
        name="fox_attention",
    )(z, z, z, qfeat, kfeat)


def _ca_kernel(q_ref, k_ref, v_ref, u_ref, o_ref, vaug_ref, s_ref, mb_ref, *, blk):
    S = q_ref.shape[0]
    left = CA_WINDOW_BLOCKS - 1
    n_blk = S // blk
    kblocks = [[(j - left + i, i) for i in range(CA_WINDOW_BLOCKS) if j - left + i >= 0]
               for j in range(n_blk)]

    for r in range(n_blk):
        rows = slice(r * blk, (r + 1) * blk)
        vf = v_ref[rows, :].astype(F32)
        for hh in range(HEAD_PAIR):
            vaug_ref[hh, rows, :] = jnp.where(_head_lane_mask(hh, vf.shape), vf, 1.0).astype(BF16)

    for j in range(n_blk):
        qf = q_ref[j * blk:(j + 1) * blk, :].astype(F32) * (CA_HEAD_DIM ** -0.5 * LOG2E)
        for hh in range(HEAD_PAIR):
            qm = jnp.where(_head_lane_mask(hh, qf.shape), qf, 0.0).astype(BF16)
            ml = None
            for kb, i in kblocks[j]:
                s = _dot_nt(qm, k_ref[kb * blk:(kb + 1) * blk, :]) + u_ref[hh, i]
                s_ref[j, hh, i] = s
                for g in range(blk // LANES):
                    sg = s[:, g * LANES:(g + 1) * LANES]
                    ml = sg if ml is None else jnp.maximum(ml, sg)
            mb_ref[j, hh] = jnp.broadcast_to(jnp.max(ml, axis=1, keepdims=True), (blk, LANES))

    for j in range(n_blk):
        outs = []
        for hh in range(HEAD_PAIR):
            mb = jnp.concatenate([mb_ref[j, hh]] * (blk // LANES), axis=1)
            acc = None
            for kb, i in kblocks[j]:
                pr = jnp.exp2((s_ref[j, hh, i] - mb).astype(BF16))
                d = _dot(pr, vaug_ref[hh, kb * blk:(kb + 1) * blk, :])
                acc = d if acc is None else acc + d
            outs.append(acc / pltpu.roll(acc, CA_HEAD_DIM, 1))
        o_ref[j * blk:(j + 1) * blk, :] = jnp.where(
            _head_lane_mask(0, outs[0].shape), outs[0], outs[1]).astype(o_ref.dtype)


def _ca_table(rel_bias, blk):
    H = rel_bias.shape[0]
    n_m = CA_WINDOW_BLOCKS * blk
    shift = (CA_WINDOW_BLOCKS - 1) * blk
    qi = np.arange(blk)[:, None]
    k_rel = np.arange(n_m)[None, :] - shift
    dchunk = qi // CHUNK - np.floor_divide(k_rel, CHUNK)
    valid = (dchunk >= 0) & (dchunk <= CA_LEFT_CHUNKS)
    period = n_m + blk
    x = np.arange(period)
    x = np.where(x >= n_m, x - period, x)
    vec = rel_bias.astype(F32)[:, np.clip(shift - x, REL_MIN, REL_MAX) - REL_MIN]
    toep = jnp.tile(vec, (1, blk))[:, :blk * (period - 1)].reshape(H, blk, period - 1)[:, :, :n_m]
    table = jnp.where(jnp.asarray(valid)[None], toep * LOG2E, NEG_INF)
    return table.reshape(H, blk, CA_WINDOW_BLOCKS, blk).transpose(0, 2, 1, 3)


def _chunk_attention(z, table, base):
    B, S, _ = z.shape
    blk = ATT_BLOCK
    n_pair = CA_WIDTH // LANES
    return pl.pallas_call(
        functools.partial(_ca_kernel, blk=blk),
        out_shape=jax.ShapeDtypeStruct((B, S, CA_WIDTH), BF16),
        grid=(n_pair, B),
        in_specs=[pl.BlockSpec((None, S, LANES), lambda p, b: (b, 0, base + p)),
                  pl.BlockSpec((None, S, LANES), lambda p, b: (b, 0, base + n_pair + p)),
                  pl.BlockSpec((None, S, LANES), lambda p, b: (b, 0, base + 2 * n_pair + p)),
                  pl.BlockSpec((HEAD_PAIR, CA_WINDOW_BLOCKS, blk, blk), lambda p, b: (p, 0, 0, 0))],
        out_specs=pl.BlockSpec((None, S, LANES), lambda p, b: (b, 0, p)),
        scratch_shapes=[pltpu.VMEM((HEAD_PAIR, S, LANES), BF16),
                        pltpu.VMEM((S // blk, HEAD_PAIR, CA_WINDOW_BLOCKS, blk, blk), F32),
                        pltpu.VMEM((S // blk, HEAD_PAIR, blk, LANES), F32)],
        compiler_params=_params(2),
        name="chunk_attention",
    )(z, z, z, table)


def _s5_kernel(x_ref, wsu_ref, bsu_ref, bbr_ref, bbi_ref, cc_ref, ar_ref, ai_ref, d_ref, wg_ref, o_ref,
               ut_ref, xr_ref, xi_ref, yt_ref, hr_ref, hi_ref, *, steps, batch):
    @pl.when(pl.program_id(0) == 0)
    def _init():
        hr_ref[...] = jnp.zeros_like(hr_ref)
        hi_ref[...] = jnp.zeros_like(hi_ref)

    for b in range(batch):
        su = _dot(x_ref[b], wsu_ref[...]) + bsu_ref[...]
        for i in range(SSM_SLABS):
            ut_ref[i, pl.ds(b, steps, stride=batch), :] = su[:, i * LANES:(i + 1) * LANES]

    for i in range(SSM_SLABS):
        ui = ut_ref[i].astype(BF16)
        cols = slice(i * SSM_SLAB_STATE, (i + 1) * SSM_SLAB_STATE)
        xr_ref[:, cols] = _dot(ui, bbr_ref[i])
        xi_ref[:, cols] = _dot(ui, bbi_ref[i])

    for i in range(SSM_SLABS):
        cols = slice(i * SSM_SLAB_STATE, (i + 1) * SSM_SLAB_STATE)
        ar = ar_ref[:, cols]
        ai = ai_ref[:, cols]

        def body(t, carry, cols=cols, ar=ar, ai=ai):
            hr, hi = carry
            r0 = pl.multiple_of(t * batch, batch)
            nhr = ar * hr - ai * hi + xr_ref[pl.ds(r0, batch), cols]
            nhi = ar * hi + ai * hr + xi_ref[pl.ds(r0, batch), cols]
            xr_ref[pl.ds(r0, batch), cols] = nhr
            xi_ref[pl.ds(r0, batch), cols] = nhi
            return nhr, nhi

        hr, hi = lax.fori_loop(0, steps, body, (hr_ref[:, cols], hi_ref[:, cols]), unroll=8)
        hr_ref[:, cols] = hr
        hi_ref[:, cols] = hi

    ys = []
    for i in range(SSM_SLABS):
        cols = slice(i * SSM_SLAB_STATE, (i + 1) * SSM_SLAB_STATE)
        h = jnp.concatenate([xr_ref[:, cols].astype(BF16), xi_ref[:, cols].astype(BF16)], axis=1)
        ys.append(_dot(h, cc_ref[i]) + d_ref[:, i * LANES:(i + 1) * LANES] * ut_ref[i])
    z = _dot(jax.nn.gelu(jnp.concatenate(ys, axis=1)).astype(BF16), wg_ref[...])
    half = z.shape[1] // 2
    out = z[:, :half] * jax.nn.sigmoid(z[:, half:])
    for k in range(half // LANES):
        yt_ref[k] = out[:, k * LANES:(k + 1) * LANES]
    for b in range(batch):
        for k in range(half // LANES):
            o_ref[b, :, k * LANES:(k + 1) * LANES] = (
                yt_ref[k, pl.ds(b, steps, stride=batch), :].astype(o_ref.dtype))


def _s5_branch(xb, w_su, b_su, bbr, bbi, cc, ar, ai, d, w_glu):
    B, S, D = xb.shape
    steps = min(SSM_TIME_BLOCK, S)
    tr = steps * B
    n_state = SSM_GROUPS * SSM_STATE
    d_out = w_glu.shape[1] // 2
    full = lambda a: pl.BlockSpec(a.shape, lambda t: (0,) * a.ndim)
    return pl.pallas_call(
        functools.partial(_s5_kernel, steps=steps, batch=B),
        out_shape=jax.ShapeDtypeStruct((B, S, d_out), BF16),
        grid=(S // steps,),
        in_specs=[pl.BlockSpec((B, steps, D), lambda t: (0, t, 0)),
                  full(w_su), full(b_su), full(bbr), full(bbi), full(cc), full(ar), full(ai),
                  full(d), full(w_glu)],
        out_specs=pl.BlockSpec((B, steps, d_out), lambda t: (0, t, 0)),
        scratch_shapes=[pltpu.VMEM((SSM_SLABS, tr, LANES), F32),
                        pltpu.VMEM((tr, n_state), F32), pltpu.VMEM((tr, n_state), F32),
                        pltpu.VMEM((d_out // LANES, tr, LANES), F32),
                        pltpu.VMEM((B, n_state), F32), pltpu.VMEM((B, n_state), F32)],
        compiler_params=_params(1),
        name="s5_branch",
    )(xb, w_su, b_su, bbr, bbi, cc, ar, ai, d, w_glu)


def _s5_params(lam_re, lam_im, log_dt, b_re, b_im, c_re, c_im, batch):
    lr = jnp.minimum(lam_re.astype(F32), -1e-4)
    li = lam_im.astype(F32)
    dt = jnp.exp(log_dt.astype(F32))[:, None]
    mag = jnp.exp(lr * dt)
    ar = mag * jnp.cos(li * dt)
    ai = mag * jnp.sin(li * dt)
    den = lr * lr + li * li
    gr = ((ar - 1.0) * lr + ai * li) / den
    gi = (ai * lr - (ar - 1.0) * li) / den
    br, bi = b_re.astype(F32), b_im.astype(F32)
    bbr = gr[..., None] * br - gi[..., None] * bi
    bbi = gr[..., None] * bi + gi[..., None] * br
    eye = jnp.eye(SSM_SLAB_GROUPS, dtype=F32)

    def in_map(bb):
        bb = bb.reshape(SSM_SLABS, SSM_SLAB_GROUPS, SSM_STATE, SSM_GROUP)
        return jnp.einsum('sgpc,gh->sgchp', bb, eye).reshape(SSM_SLABS, LANES, SSM_SLAB_STATE).astype(BF16)

    def out_map(c):
        c = c.astype(F32).reshape(SSM_SLABS, SSM_SLAB_GROUPS, SSM_GROUP, SSM_STATE)
        return jnp.einsum('sgcp,gh->sgphc', c, eye).reshape(SSM_SLABS, SSM_SLAB_STATE, LANES)

    cc = jnp.concatenate([out_map(c_re), -out_map(c_im)], axis=1).astype(BF16)
    n_state = SSM_GROUPS * SSM_STATE
    ar_b = jnp.broadcast_to(ar.reshape(1, n_state), (batch, n_state))
    ai_b = jnp.broadcast_to(ai.reshape(1, n_state), (batch, n_state))
    return in_map(bbr), in_map(bbi), cc, ar_b, ai_b


def _merge_kernel(x_ref, xb_ref, ya_ref, yb_ref, yc_ref, wg_ref, bg_ref, wa_ref, wc_ref, wo_ref,
                  lg_ref, lb_ref, of_ref, ob_ref, m_ref, *, alpha):
    d = x_ref.shape[1]
    xb = xb_ref[...]
    ya = ya_ref[...]
    yc = yc_ref[...]
    for c in range(d // MERGE_SUB_COLS):
        cols = slice(c * MERGE_SUB_COLS, (c + 1) * MERGE_SUB_COLS)
        def gate(k, cols=cols):
            gcols = slice(k * d + cols.start, k * d + cols.stop)
            return jax.nn.sigmoid(_dot(xb, wg_ref[:, gcols]) + bg_ref[:, gcols])

        merged = (gate(0) * _dot(ya, wa_ref[:, cols])
                  + gate(1) * yb_ref[:, cols].astype(F32)
                  + gate(2) * _dot(yc, wc_ref[:, cols]))
        m_ref[:, cols] = merged.astype(BF16)
    h = _dot(m_ref[...], wo_ref[...])
    y = _layer_norm(alpha * x_ref[...] + h, lg_ref[...], lb_ref[...])
    of_ref[...] = y
    ob_ref[...] = y.astype(BF16)


def _merge(x, xb, ya, yb, yc, w_gate, b_gate, w_fox_o, w_ca_o, w_o, lg, lb, alpha, tm):
    B, S, D = x.shape
    tm = min(tm, S)
    row = lambda w: pl.BlockSpec((None, tm, w), lambda b, i: (b, i, 0))
    full = lambda a: pl.BlockSpec(a.shape, lambda b, i: (0,) * a.ndim)
    return pl.pallas_call(
        functools.partial(_merge_kernel, alpha=alpha),
        out_shape=(jax.ShapeDtypeStruct((B, S, D), F32), jax.ShapeDtypeStruct((B, S, D), BF16)),
        grid=(B, S // tm),
        in_specs=[row(D), row(D), row(ya.shape[2]), row(D), row(yc.shape[2]),
                  full(w_gate), full(b_gate), full(w_fox_o), full(w_ca_o), full(w_o), full(lg), full(lb)],
        out_specs=(row(D), row(D)),
        scratch_shapes=[pltpu.VMEM((tm, D), BF16)],
        compiler_params=_params(2),
        name="merge_proj_ln",
    )(x, xb, ya, yb, yc, w_gate, b_gate, w_fox_o, w_ca_o, w_o, lg, lb)


def _xattn_kernel(x_ref, xb_ref, kv_ref, wq_ref, wo_ref, lg_ref, lb_ref, of_ref, ob_ref, *, alpha):
    d = x_ref.shape[1]
    dh = d // XA_HEADS
    xb = xb_ref[...]
    heads = range(XA_HEADS)
    head = lambda h: slice(h * dh, (h + 1) * dh)
    q = [(_dot(xb, wq_ref[:, head(h)]) * (dh ** -0.5 * LOG2E)).astype(BF16) for h in heads]
    s = [_dot_nt(q[h], kv_ref[:, head(h)]) for h in heads]
    m = [jnp.max(s[h], axis=-1, keepdims=True) for h in heads]
    pr = [jnp.exp2(s[h] - m[h]) for h in heads]
    l = [jnp.sum(pr[h], axis=-1, keepdims=True) for h in heads]
    outs = [(_dot(pr[h].astype(BF16), kv_ref[:, d + h * dh:d + (h + 1) * dh]) / l[h]).astype(BF16)
            for h in heads]
    hout = _dot(jnp.concatenate(outs, axis=1), wo_ref[...])
    y = _layer_norm(alpha * x_ref[...] + hout, lg_ref[...], lb_ref[...])
    of_ref[...] = y
    ob_ref[...] = y.astype(BF16)


def _cross_attention(x, xb, kv, wq, wo, lg, lb, alpha, tm):
    B, S, D = x.shape
    M = kv.shape[1]
    tm = min(tm, S)
    row = pl.BlockSpec((None, tm, D), lambda b, i: (b, i, 0))
    full = lambda a: pl.BlockSpec(a.shape, lambda b, i: (0,) * a.ndim)
    return pl.pallas_call(
        functools.partial(_xattn_kernel, alpha=alpha),
        out_shape=(jax.ShapeDtypeStruct((B, S, D), F32), jax.ShapeDtypeStruct((B, S, D), BF16)),
        grid=(B, S // tm),
        in_specs=[row, row, pl.BlockSpec((None, M, 2 * D), lambda b, i: (b, 0, 0)),
                  full(wq), full(wo), full(lg), full(lb)],
        out_specs=(row, row),
        compiler_params=_params(2),
        name="cross_attention_ln",
    )(x, xb, kv, wq, wo, lg, lb)


def _ffn_kernel(x_ref, xb_ref, wup_ref, cw_ref, cb_ref, wdn_ref, lg_ref, lb_ref, of_ref, ob_ref,
                abuf_ref, act_ref, *, alpha, d_ff):
    tm = x_ref.shape[0]
    pad = FFN_HALO_ROWS

    @pl.when(pl.program_id(1) == 0)
    def _seq_start():
        abuf_ref[0:pad, :] = jnp.zeros((pad, d_ff), F32)

    xb = xb_ref[...]
    h = jnp.zeros(x_ref.shape, F32)
    n_sub = d_ff // FFN_SUB_COLS
    for j in range(n_sub):
        cols = slice(j * FFN_SUB_COLS, (j + 1) * FFN_SUB_COLS)
        gcols = slice(d_ff + j * FFN_SUB_COLS, d_ff + (j + 1) * FFN_SUB_COLS)
        a = _dot(xb, wup_ref[:, cols])
        g = _dot(xb, wup_ref[:, gcols])
        abuf_ref[pad:pad + tm, cols] = a
        a1 = abuf_ref[pad - 1:pad - 1 + tm, cols]
        a2 = abuf_ref[pad - 2:pad - 2 + tm, cols]
        conv = (cw_ref[0:1, cols] * a2 + cw_ref[1:2, cols] * a1 + cw_ref[2:3, cols] * a
                + cb_ref[:, cols])
        act_ref[:, cols] = (jax.nn.gelu(conv) * g).astype(BF16)
        if (j + 1) % FFN_DOWN_GROUP == 0 or j == n_sub - 1:
            lo = (j // FFN_DOWN_GROUP) * FFN_DOWN_GROUP * FFN_SUB_COLS
            rows = slice(lo, (j + 1) * FFN_SUB_COLS)
            h = h + _dot(act_ref[:, rows], wdn_ref[rows, :])
    abuf_ref[0:pad, :] = abuf_ref[tm:tm + pad, :]
    y = _layer_norm(alpha * x_ref[...] + h, lg_ref[...], lb_ref[...])
    of_ref[...] = y
    ob_ref[...] = y.astype(BF16)


def _conv_ffn(x, xb, w_up, conv_w, conv_b, w_down, lg, lb, alpha, tm):
    B, S, D = x.shape
    d_ff = w_down.shape[0]
    tm = min(tm, S)
    row = pl.BlockSpec((None, tm, D), lambda b, i: (b, i, 0))
    full = lambda a: pl.BlockSpec(a.shape, lambda b, i: (0,) * a.ndim)
    once = lambda a: pl.BlockSpec(a.shape, lambda b, i: (0,) * a.ndim, pipeline_mode=pl.Buffered(1))
    return pl.pallas_call(
        functools.partial(_ffn_kernel, alpha=alpha, d_ff=d_ff),
        out_shape=(jax.ShapeDtypeStruct((B, S, D), F32), jax.ShapeDtypeStruct((B, S, D), BF16)),
        grid=(B, S // tm),
        in_specs=[row, row, once(w_up), full(conv_w), full(conv_b), once(w_down), full(lg), full(lb)],
        out_specs=(row, row),
        scratch_shapes=[pltpu.VMEM((tm + FFN_HALO_ROWS, d_ff), F32), pltpu.VMEM((tm, d_ff), BF16)],
        compiler_params=_params(2),
        name="conv_ffn_ln",
    )(x, xb, w_up, conv_w, conv_b, w_down, lg, lb)


def _split_in_proj(w_in, b_in, d_model):
    o = np.cumsum([0, FOX_WIDTH, FOX_WIDTH, FOX_WIDTH, FOX_HEADS, SSM_WIDTH,
                   CA_WIDTH, CA_WIDTH, CA_WIDTH, 3 * d_model])
    fqkv, ff, su, cqkv, gates = (slice(o[0], o[3]), slice(o[3], o[4]), slice(o[4], o[5]),
                                 slice(o[5], o[8]), slice(o[8], o[9]))
    cols = lambda s: (w_in[:, s], b_in[s])
    w_qkv = jnp.concatenate([w_in[:, fqkv], w_in[:, cqkv]], axis=1)
    b_qkv = jnp.concatenate([b_in[fqkv], b_in[cqkv]])
    w_ff, b_ff = cols(ff)
    pad = LANES - FOX_HEADS
    w_ff = jnp.pad(w_ff, ((0, 0), (0, pad)))
    b_ff = jnp.pad(b_ff, (0, pad))
    prep = lambda w, b: (w.astype(BF16), b.astype(F32)[None, :])
    return prep(w_qkv, b_qkv), prep(*cols(gates)), prep(*cols(su)), prep(w_ff, b_ff)


def kernel(x, mem, w_in, b_in, ssm_lambda_re, ssm_lambda_im, ssm_log_dt, ssm_b_re, ssm_b_im, ssm_c_re, ssm_c_im, ssm_d, ca_rel_bias, w_fox_o, w_ssm_glu, w_ca_o, w_o, xa_wq, xa_wkv, xa_wo, ffn_w_up, ffn_conv_w, ffn_conv_b, ffn_w_down, ln_g, ln_b):
    B, S, D = x.shape
    depth = w_in.shape[0]
    d_ff = ffn_w_down.shape[1]
    alpha = float((2 * depth) ** 0.25)
    mem_b = mem.astype(BF16)
    xf = x.astype(F32)
    xb = x.astype(BF16)
    zero_bias = jnp.zeros((1, 2 * D), F32)
    for l in range(depth):
        (w_qkv, b_qkv), (w_gate, b_gate), (w_su, b_su), (w_ff, b_ff) = _split_in_proj(w_in[l], b_in[l], D)
        z = _matmul(xb, w_qkv, b_qkv, BF16, tm=512, name="in_proj")
        qfeat, kfeat = _forget_decay_features(xb, w_ff, b_ff)
        ya = _fox_attention(z, qfeat, kfeat, 0)
        yc = _chunk_attention(z, _ca_table(ca_rel_bias[l], ATT_BLOCK), 3 * FOX_WIDTH // LANES)
        bbr, bbi, cc, ar, ai = _s5_params(ssm_lambda_re[l], ssm_lambda_im[l], ssm_log_dt[l],
                                          ssm_b_re[l], ssm_b_im[l], ssm_c_re[l], ssm_c_im[l], B)
        yb = _s5_branch(xb, w_su, b_su, bbr, bbi, cc, ar, ai,
                        ssm_d[l].astype(F32)[None, :], w_ssm_glu[l].astype(BF16))
        ln = lambda k: (ln_g[l, k].astype(F32)[None, :], ln_b[l, k].astype(F32)[None, :])
        xf, xb = _merge(xf, xb, ya, yb, yc, w_gate, b_gate, w_fox_o[l].astype(BF16),
                        w_ca_o[l].astype(BF16), w_o[l].astype(BF16), *ln(0), alpha, tm=512)
        kv = _matmul(mem_b, xa_wkv[l].astype(BF16), zero_bias, BF16, tm=256, name="mem_kv")
        xf, xb = _cross_attention(xf, xb, kv, xa_wq[l].astype(BF16), xa_wo[l].astype(BF16),
                                  *ln(1), alpha, tm=512)
        xf, xb = _conv_ffn(xf, xb, ffn_w_up[l].astype(BF16), ffn_conv_w[l].astype(F32),
                           ffn_conv_b[l].astype(F32)[None, :], ffn_w_down[l].astype(BF16),
                           *ln(2), alpha, tm=512)
    return xf.astype(x.dtype)
```

```python
import functools
import math

import jax
import jax.numpy as jnp
import numpy as np
from jax import lax
from jax.experimental import pallas as pl
from jax.experimental.pallas import tpu as pltpu

F32 = jnp.float32
BF16 = jnp.bfloat16

CHUNK = 64
Q_BLOCK = 128
FOX_HEADS = 8
FOX_HEAD_DIM = 64
FOX_WIDTH = FOX_HEADS * FOX_HEAD_DIM
SSM_GROUP = 16
SSM_WIDTH = 512
SSM_GROUPS = SSM_WIDTH // SSM_GROUP
SSM_STATE = 64
CA_HEADS = 8
CA_HEAD_DIM = 64
CA_WIDTH = CA_HEADS * CA_HEAD_DIM
CA_LEFT_CHUNKS = 8
REL_MIN = -(CHUNK - 1)
REL_MAX = 4 * CHUNK
XA_HEADS = 4
CONV_WIDTH = 3
LN_EPS = 1e-5
NEG_INF = -1e30

LOG2E = math.log2(math.e)
DECAY_LANES = 6
DECAY_STRIDE = 8
LANES = 128
HEAD_PAIR = LANES // FOX_HEAD_DIM
ATT_BLOCK = 256
FOX_BLOCK = 512
CA_WINDOW_BLOCKS = 3
SSM_SLAB_GROUPS = LANES // SSM_GROUP
SSM_SLABS = SSM_GROUPS // SSM_SLAB_GROUPS
SSM_SLAB_STATE = SSM_SLAB_GROUPS * SSM_STATE
SSM_TIME_BLOCK = 128
MATMUL_SUB_COLS = 512
MERGE_SUB_COLS = 256
FFN_SUB_COLS = 256
FFN_DOWN_GROUP = 4
FFN_HALO_ROWS = 8
VMEM_LIMIT = 56 * 1024 * 1024

_ARB = "arbitrary"


def _params(n_axes):
    return pltpu.CompilerParams(dimension_semantics=(_ARB,) * n_axes,
                                vmem_limit_bytes=VMEM_LIMIT)


def _layer_norm(v, g, b):
    mu = jnp.mean(v, axis=-1, keepdims=True)
    d = v - mu
    var = jnp.mean(d * d, axis=-1, keepdims=True)
    return d * lax.rsqrt(var + LN_EPS) * g + b


def _dot(a, b):
    return jnp.dot(a, b, preferred_element_type=F32)


def _dot_nt(a, b):
    return lax.dot_general(a, b, (((1,), (1,)), ((), ())), preferred_element_type=F32)


def _mm_kernel(x_ref, w_ref, b_ref, o_ref):
    x = x_ref[...]
    n = w_ref.shape[1]
    sub = min(MATMUL_SUB_COLS, n)
    for c in range(n // sub):
        cols = slice(c * sub, (c + 1) * sub)
        o_ref[:, cols] = (_dot(x, w_ref[:, cols]) + b_ref[:, cols]).astype(o_ref.dtype)


def _matmul(x, w, bias, out_dtype, *, tm, name="matmul"):
    B, S, K = x.shape
    N = w.shape[1]
    tm = min(tm, S)
    return pl.pallas_call(
        _mm_kernel,
        out_shape=jax.ShapeDtypeStruct((B, S, N), out_dtype),
        grid=(B, S // tm),
        in_specs=[pl.BlockSpec((None, tm, K), lambda b, i: (b, i, 0)),
                  pl.BlockSpec((K, N), lambda b, i: (0, 0)),
                  pl.BlockSpec((1, N), lambda b, i: (0, 0))],
        out_specs=pl.BlockSpec((None, tm, N), lambda b, i: (b, i, 0)),
        compiler_params=_params(2),
        name=name,
    )(x, w, bias)


def _split3(c):
    hi = c.astype(BF16).astype(F32)
    r = c - hi
    mid = r.astype(BF16).astype(F32)
    lo = (r - mid).astype(BF16).astype(F32)
    return hi, mid, lo


def _decay_scatter(query_side):
    r = lax.broadcasted_iota(jnp.int32, (3 * LANES, LANES), 0)
    c = lax.broadcasted_iota(jnp.int32, (3 * LANES, LANES), 1)
    piece, h = r // LANES, r % LANES
    dst = DECAY_STRIDE * h + piece + (0 if query_side else 3)
    mat = jnp.where((h < FOX_HEADS) & (c == dst), 1.0 if query_side else -1.0, 0.0).astype(BF16)
    lane = lax.broadcasted_iota(jnp.int32, (1, LANES), 1)
    k = lane % DECAY_STRIDE
    ones_lo = 3 if query_side else 0
    ones = jnp.where((lane < DECAY_STRIDE * FOX_HEADS) & (k >= ones_lo) & (k < ones_lo + 3), 1.0, 0.0)
    return mat, ones


def _cum_kernel(x_ref, w_ref, b_ref, qf_ref, kf_ref, *, blk):
    S = x_ref.shape[0]
    row = lax.broadcasted_iota(jnp.int32, (blk, blk), 0)
    col = lax.broadcasted_iota(jnp.int32, (blk, blk), 1)
    tri = jnp.where(row >= col, 1.0, 0.0).astype(BF16)
    q_mat, q_ones = _decay_scatter(True)
    k_mat, k_ones = _decay_scatter(False)
    local = []
    for i in range(S // blk):
        x = _dot(x_ref[i * blk:(i + 1) * blk, :], w_ref[...]) + b_ref[...]
        ls = jnp.minimum(x, 0.0) - jnp.log(1.0 + jnp.exp(-jnp.abs(x)))
        hi, mid, lo = _split3(ls)
        local.append(_dot(tri, hi.astype(BF16)) + _dot(tri, mid.astype(BF16))
                     + _dot(tri, lo.astype(BF16)))
    carry = jnp.zeros((1, LANES), F32)
    for i in range(S // blk):
        cs = local[i] + carry
        carry = cs[blk - 1:blk, :]
        pieces = jnp.concatenate([v.astype(BF16) for v in _split3(cs * LOG2E)], axis=1)
        qf_ref[i * blk:(i + 1) * blk, :] = _dot(pieces, q_mat) + q_ones
        kf_ref[i * blk:(i + 1) * blk, :] = _dot(pieces, k_mat) + k_ones


def _forget_decay_features(xb, w_ff, b_ff):
    B, S, D = xb.shape
    L = w_ff.shape[1]
    out = jax.ShapeDtypeStruct((B, S, L), F32)
    spec = pl.BlockSpec((None, S, L), lambda b: (b, 0, 0))
    return pl.pallas_call(
        functools.partial(_cum_kernel, blk=min(ATT_BLOCK, S)),
        out_shape=(out, out),
        grid=(B,),
        in_specs=[pl.BlockSpec((None, S, D), lambda b: (b, 0, 0)),
                  pl.BlockSpec((D, L), lambda b: (0, 0)),
                  pl.BlockSpec((1, L), lambda b: (0, 0))],
        out_specs=(spec, spec),
        compiler_params=_params(1),
        name="forget_cumsum",
    )(xb, w_ff, b_ff)


def _head_lane_mask(j, shape):
    lane = lax.broadcasted_iota(jnp.int32, shape, len(shape) - 1)
    lo = j * FOX_HEAD_DIM
    return (lane >= lo) & (lane < lo + FOX_HEAD_DIM)


def _with_decay_lanes(x, feat, j, p):
    a0 = (1 - j) * FOX_HEAD_DIM
    shift = a0 - DECAY_STRIDE * j + (LANES if j else 0) - DECAY_STRIDE * HEAD_PAIR * p
    lane = lax.broadcasted_iota(jnp.int32, x.shape, 1)
    window = (lane >= a0) & (lane < a0 + DECAY_LANES)
    aug = jnp.where(window, pltpu.roll(feat, shift, 1), 0.0)
    return jnp.where(_head_lane_mask(j, x.shape), x, aug).astype(BF16)


def _fox_kernel(q_ref, k_ref, v_ref, qf_ref, kf_ref, o_ref, kaug_ref, vaug_ref, s_ref, mb_ref, *, blk):
    p = pl.program_id(1)
    S = k_ref.shape[0]
    n_blk = S // blk
    tri = lambda qi, c: qi * (qi + 1) // 2 + c

    for r in range(n_blk):
        rows = slice(r * blk, (r + 1) * blk)
        kf = k_ref[rows, :].astype(F32)
        vf = v_ref[rows, :].astype(F32)
        feat = kf_ref[rows, :]
        for j in range(HEAD_PAIR):
            kaug_ref[j, rows, :] = _with_decay_lanes(kf, feat, j, p)
            vaug_ref[j, rows, :] = jnp.where(_head_lane_mask(j, vf.shape), vf, 1.0).astype(BF16)

    for qi in range(n_blk):
        rows = slice(qi * blk, (qi + 1) * blk)
        qf = q_ref[rows, :].astype(F32) * (FOX_HEAD_DIM ** -0.5 * LOG2E)
        feat = qf_ref[rows, :]
        for j in range(HEAD_PAIR):
            qaug = _with_decay_lanes(qf, feat, j, p)
            ml = None
            for c in range(qi + 1):
                s = _dot_nt(qaug, kaug_ref[j, c * blk:(c + 1) * blk, :])
                if c == qi:
                    row = lax.broadcasted_iota(jnp.int32, (blk, blk), 0)
                    colk = lax.broadcasted_iota(jnp.int32, (blk, blk), 1)
                    s = jnp.where(row >= colk, s, NEG_INF)
                s_ref[tri(qi, c), j] = s
                for g in range(blk // LANES):
                    sg = s[:, g * LANES:(g + 1) * LANES]
                    ml = sg if ml is None else jnp.maximum(ml, sg)
            mb_ref[qi, j] = jnp.broadcast_to(jnp.max(ml, axis=1, keepdims=True), (blk, LANES))

    for qi in range(n_blk):
        outs = []
        for j in range(HEAD_PAIR):
            mb = jnp.concatenate([mb_ref[qi, j]] * (blk // LANES), axis=1)
            acc = None
            for c in range(qi + 1):
                pr = jnp.exp2((s_ref[tri(qi, c), j] - mb).astype(BF16))
                d = _dot(pr, vaug_ref[j, c * blk:(c + 1) * blk, :])
                acc = d if acc is None else acc + d
            outs.append(acc / pltpu.roll(acc, FOX_HEAD_DIM, 1))
        o_ref[qi * blk:(qi + 1) * blk, :] = jnp.where(
            _head_lane_mask(0, outs[0].shape), outs[0], outs[1]).astype(o_ref.dtype)


def _fox_attention(z, qfeat, kfeat, base):
    B, S, _ = z.shape
    blk = min(FOX_BLOCK, S)
    n_q = S // blk
    n_pair = FOX_WIDTH // LANES
    return pl.pallas_call(
        functools.partial(_fox_kernel, blk=blk),
        out_shape=jax.ShapeDtypeStruct((B, S, FOX_WIDTH), BF16),
        grid=(B, n_pair),
        in_specs=[pl.BlockSpec((None, S, LANES), lambda b, p: (b, 0, base + p)),
                  pl.BlockSpec((None, S, LANES), lambda b, p: (b, 0, base + n_pair + p)),
                  pl.BlockSpec((None, S, LANES), lambda b, p: (b, 0, base + 2 * n_pair + p)),
                  pl.BlockSpec((None, S, LANES), lambda b, p: (b, 0, 0)),
                  pl.BlockSpec((None, S, LANES), lambda b, p: (b, 0, 0))],
        out_specs=pl.BlockSpec((None, S, LANES), lambda b, p: (b, 0, p)),
        scratch_shapes=[pltpu.VMEM((HEAD_PAIR, S, LANES), BF16),
                        pltpu.VMEM((HEAD_PAIR, S, LANES), BF16),
                        pltpu.VMEM((n_q * (n_q + 1) // 2, HEAD_PAIR, blk, blk), F32),
                        pltpu.VMEM((n_q, HEAD_PAIR, blk, LANES), F32)],
        compiler_params=_params(2),
        name="fox_attention",
    )(z, z, z, qfeat, kfeat)


def _ca_kernel(q_ref, k_ref, v_ref, vec_ref, o_ref, u_ref, vaug_ref, s_ref, mb_ref, *, blk):
    S = q_ref.shape[0]
    left = CA_WINDOW_BLOCKS - 1
    n_blk = S // blk
    kblocks = [[(j - left + i, i) for i in range(CA_WINDOW_BLOCKS) if j - left + i >= 0]
               for j in range(n_blk)]

    @pl.when(pl.program_id(1) == 0)
    def _build_table():
        row_chunk = lax.broadcasted_iota(jnp.int32, (blk, blk), 0) // CHUNK
        col_chunk = lax.broadcasted_iota(jnp.int32, (blk, blk), 1) // CHUNK
        for hh in range(HEAD_PAIR):
            vec = jnp.broadcast_to(vec_ref[hh:hh + 1, :], (blk, vec_ref.shape[1]))
            toep = pltpu.roll(vec, 0, 1, stride=1, stride_axis=0)
            for i in range(CA_WINDOW_BLOCKS):
                dchunk = row_chunk - col_chunk + (left - i) * (blk // CHUNK)
                valid = (dchunk >= 0) & (dchunk <= CA_LEFT_CHUNKS)
                u_ref[hh, i] = jnp.where(valid, toep[:, i * blk:(i + 1) * blk], NEG_INF)

    for r in range(n_blk):
        rows = slice(r * blk, (r + 1) * blk)
        vf = v_ref[rows, :].astype(F32)
        for hh in range(HEAD_PAIR):
            vaug_ref[hh, rows, :] = jnp.where(_head_lane_mask(hh, vf.shape), vf, 1.0).astype(BF16)

    for j in range(n_blk):
        qf = q_ref[j * blk:(j + 1) * blk, :].astype(F32) * (CA_HEAD_DIM ** -0.5 * LOG2E)
        for hh in range(HEAD_PAIR):
            qm = jnp.where(_head_lane_mask(hh, qf.shape), qf, 0.0).astype(BF16)
            ml = None
            for kb, i in kblocks[j]:
                s = _dot_nt(qm, k_ref[kb * blk:(kb + 1) * blk, :]) + u_ref[hh, i]
                s_ref[j, hh, i] = s
                for g in range(blk // LANES):
                    sg = s[:, g * LANES:(g + 1) * LANES]
                    ml = sg if ml is None else jnp.maximum(ml, sg)
            mb_ref[j, hh] = jnp.broadcast_to(jnp.max(ml, axis=1, keepdims=True), (blk, LANES))

    for j in range(n_blk):
        outs = []
        for hh in range(HEAD_PAIR):
            mb = jnp.concatenate([mb_ref[j, hh]] * (blk // LANES), axis=1)
            acc = None
            for kb, i in kblocks[j]:
                pr = jnp.exp2((s_ref[j, hh, i] - mb).astype(BF16))
                d = _dot(pr, vaug_ref[hh, kb * blk:(kb + 1) * blk, :])
                acc = d if acc is None else acc + d
            outs.append(acc / pltpu.roll(acc, CA_HEAD_DIM, 1))
        o_ref[j * blk:(j + 1) * blk, :] = jnp.where(
            _head_lane_mask(0, outs[0].shape), outs[0], outs[1]).astype(o_ref.dtype)


def _ca_bias_vector(rel_bias, blk):
    n_m = CA_WINDOW_BLOCKS * blk
    shift = (CA_WINDOW_BLOCKS - 1) * blk
    period = n_m + blk
    x = np.arange(period)
    x = np.where(x >= n_m, x - period, x)
    vec = rel_bias.astype(F32)[:, np.clip(shift - x, REL_MIN, REL_MAX) - REL_MIN] * LOG2E
    return vec.reshape(rel_bias.shape[0] // HEAD_PAIR, HEAD_PAIR, period)


def _chunk_attention(z, bias_vec, base):
    B, S, _ = z.shape
    blk = ATT_BLOCK
    n_pair = CA_WIDTH // LANES
    period = bias_vec.shape[2]
    return pl.pallas_call(
        functools.partial(_ca_kernel, blk=blk),
        out_shape=jax.ShapeDtypeStruct((B, S, CA_WIDTH), BF16),
        grid=(n_pair, B),
        in_specs=[pl.BlockSpec((None, S, LANES), lambda p, b: (b, 0, base + p)),
                  pl.BlockSpec((None, S, LANES), lambda p, b: (b, 0, base + n_pair + p)),
                  pl.BlockSpec((None, S, LANES), lambda p, b: (b, 0, base + 2 * n_pair + p)),
                  pl.BlockSpec((None, HEAD_PAIR, period), lambda p, b: (p, 0, 0))],
        out_specs=pl.BlockSpec((None, S, LANES), lambda p, b: (b, 0, p)),
        scratch_shapes=[pltpu.VMEM((HEAD_PAIR, CA_WINDOW_BLOCKS, blk, blk), F32),
                        pltpu.VMEM((HEAD_PAIR, S, LANES), BF16),
                        pltpu.VMEM((S // blk, HEAD_PAIR, CA_WINDOW_BLOCKS, blk, blk), F32),
                        pltpu.VMEM((S // blk, HEAD_PAIR, blk, LANES), F32)],
        compiler_params=_params(2),
        name="chunk_attention",
    )(z, z, z, bias_vec)


def _s5_kernel(x_ref, wsu_ref, bsu_ref, bbr_ref, bbi_ref, cc_ref, ar_ref, ai_ref, d_ref, wg_ref, o_ref,
               ut_ref, xr_ref, xi_ref, yt_ref, hr_ref, hi_ref, *, steps, batch):
    @pl.when(pl.program_id(0) == 0)
    def _init():
        hr_ref[...] = jnp.zeros_like(hr_ref)
        hi_ref[...] = jnp.zeros_like(hi_ref)

    for b in range(batch):
        su = _dot(x_ref[b], wsu_ref[...]) + bsu_ref[...]
        for i in range(SSM_SLABS):
            ut_ref[i, pl.ds(b, steps, stride=batch), :] = su[:, i * LANES:(i + 1) * LANES]

    slab = lambda i: slice(i * SSM_SLAB_STATE, (i + 1) * SSM_SLAB_STATE)
    for i in range(SSM_SLABS):
        ui = ut_ref[i].astype(BF16)
        xr_ref[:, slab(i)] = _dot(ui, bbr_ref[i])
        xi_ref[:, slab(i)] = _dot(ui, bbi_ref[i])
    for i in range(SSM_SLABS):
        cols = slab(i)
        ar = ar_ref[:, cols]
        ai = ai_ref[:, cols]
        hr = hr_ref[:, cols]
        hi = hi_ref[:, cols]
        for t in range(steps):
            rows = slice(t * batch, (t + 1) * batch)
            hr, hi = (ar * hr - ai * hi + xr_ref[rows, cols], ar * hi + ai * hr + xi_ref[rows, cols])
            xr_ref[rows, cols] = hr
            xi_ref[rows, cols] = hi
        hr_ref[:, cols] = hr
        hi_ref[:, cols] = hi
    ys = []
    for i in range(SSM_SLABS):
        cols = slab(i)
        h = jnp.concatenate([xr_ref[:, cols].astype(BF16), xi_ref[:, cols].astype(BF16)], axis=1)
        ys.append(_dot(h, cc_ref[i]) + d_ref[:, i * LANES:(i + 1) * LANES] * ut_ref[i])
    g = jax.nn.gelu(jnp.concatenate(ys, axis=1)).astype(BF16)
    half = wg_ref.shape[1] // 2
    for c in range(half // MERGE_SUB_COLS):
        cols = slice(c * MERGE_SUB_COLS, (c + 1) * MERGE_SUB_COLS)
        gcols = slice(half + c * MERGE_SUB_COLS, half + (c + 1) * MERGE_SUB_COLS)
        out = _dot(g, wg_ref[:, cols]) * jax.nn.sigmoid(_dot(g, wg_ref[:, gcols]))
        for k in range(MERGE_SUB_COLS // LANES):
            yt_ref[c * (MERGE_SUB_COLS // LANES) + k] = out[:, k * LANES:(k + 1) * LANES]
    for b in range(batch):
        for k in range(half // LANES):
            o_ref[b, :, k * LANES:(k + 1) * LANES] = (
                yt_ref[k, pl.ds(b, steps, stride=batch), :].astype(o_ref.dtype))


def _s5_branch(xb, w_su, b_su, bbr, bbi, cc, ar, ai, d, w_glu):
    B, S, D = xb.shape
    steps = min(SSM_TIME_BLOCK, S)
    tr = steps * B
    n_state = SSM_GROUPS * SSM_STATE
    d_out = w_glu.shape[1] // 2
    full = lambda a: pl.BlockSpec(a.shape, lambda t: (0,) * a.ndim)
    return pl.pallas_call(
        functools.partial(_s5_kernel, steps=steps, batch=B),
        out_shape=jax.ShapeDtypeStruct((B, S, d_out), BF16),
        grid=(S // steps,),
        in_specs=[pl.BlockSpec((B, steps, D), lambda t: (0, t, 0)),
                  full(w_su), full(b_su), full(bbr), full(bbi), full(cc), full(ar), full(ai),
                  full(d), full(w_glu)],
        out_specs=pl.BlockSpec((B, steps, d_out), lambda t: (0, t, 0)),
        scratch_shapes=[pltpu.VMEM((SSM_SLABS, tr, LANES), F32),
                        pltpu.VMEM((tr, n_state), F32), pltpu.VMEM((tr, n_state), F32),
                        pltpu.VMEM((d_out // LANES, tr, LANES), F32),
                        pltpu.VMEM((B, n_state), F32), pltpu.VMEM((B, n_state), F32)],
        compiler_params=_params(1),
        name="s5_branch",
    )(xb, w_su, b_su, bbr, bbi, cc, ar, ai, d, w_glu)


def _s5_params(lam_re, lam_im, log_dt, b_re, b_im, c_re, c_im, batch):
    lr = jnp.minimum(lam_re.astype(F32), -1e-4)
    li = lam_im.astype(F32)
    dt = jnp.exp(log_dt.astype(F32))[:, None]
    mag = jnp.exp(lr * dt)
    ar = mag * jnp.cos(li * dt)
    ai = mag * jnp.sin(li * dt)
    den = lr * lr + li * li
    gr = ((ar - 1.0) * lr + ai * li) / den
    gi = (ai * lr - (ar - 1.0) * li) / den
    br, bi = b_re.astype(F32), b_im.astype(F32)
    bbr = gr[..., None] * br - gi[..., None] * bi
    bbi = gr[..., None] * bi + gi[..., None] * br
    eye = jnp.eye(SSM_SLAB_GROUPS, dtype=F32)

    def in_map(bb):
        bb = bb.reshape(SSM_SLABS, SSM_SLAB_GROUPS, SSM_STATE, SSM_GROUP)
        return jnp.einsum('sgpc,gh->sgchp', bb, eye).reshape(SSM_SLABS, LANES, SSM_SLAB_STATE).astype(BF16)

    def out_map(c):
        c = c.astype(F32).reshape(SSM_SLABS, SSM_SLAB_GROUPS, SSM_GROUP, SSM_STATE)
        return jnp.einsum('sgcp,gh->sgphc', c, eye).reshape(SSM_SLABS, SSM_SLAB_STATE, LANES)

    cc = jnp.concatenate([out_map(c_re), -out_map(c_im)], axis=1).astype(BF16)
    n_state = SSM_GROUPS * SSM_STATE
    ar_b = jnp.broadcast_to(ar.reshape(1, n_state), (batch, n_state))
    ai_b = jnp.broadcast_to(ai.reshape(1, n_state), (batch, n_state))
    return in_map(bbr), in_map(bbi), cc, ar_b, ai_b


def _merge_kernel(x_ref, xb_ref, ya_ref, yb_ref, yc_ref, wg_ref, bg_ref, wa_ref, wc_ref, wo_ref,
                  lg_ref, lb_ref, of_ref, ob_ref, m_ref, *, alpha):
    d = x_ref.shape[1]
    xb = xb_ref[...]
    ya = ya_ref[...]
    yc = yc_ref[...]
    for c in range(d // MERGE_SUB_COLS):
        cols = slice(c * MERGE_SUB_COLS, (c + 1) * MERGE_SUB_COLS)

        def gate(k, cols=cols):
            gcols = slice(k * d + cols.start, k * d + cols.stop)
            return jax.nn.sigmoid(_dot(xb, wg_ref[:, gcols]) + bg_ref[:, gcols])

        merged = (gate(0) * _dot(ya, wa_ref[:, cols])
                  + gate(1) * yb_ref[:, cols].astype(F32)
                  + gate(2) * _dot(yc, wc_ref[:, cols]))
        m_ref[:, cols] = merged.astype(BF16)
    h = _dot(m_ref[...], wo_ref[...])
    y = _layer_norm(alpha * x_ref[...] + h, lg_ref[...], lb_ref[...])
    of_ref[...] = y
    ob_ref[...] = y.astype(BF16)


def _merge(x, xb, ya, yb, yc, w_gate, b_gate, w_fox_o, w_ca_o, w_o, lg, lb, alpha, tm):
    B, S, D = x.shape
    tm = min(tm, S)
    row = lambda w: pl.BlockSpec((None, tm, w), lambda b, i: (b, i, 0))
    full = lambda a: pl.BlockSpec(a.shape, lambda b, i: (0,) * a.ndim)
    return pl.pallas_call(
        functools.partial(_merge_kernel, alpha=alpha),
        out_shape=(jax.ShapeDtypeStruct((B, S, D), F32), jax.ShapeDtypeStruct((B, S, D), BF16)),
        grid=(B, S // tm),
        in_specs=[row(D), row(D), row(ya.shape[2]), row(D), row(yc.shape[2]),
                  full(w_gate), full(b_gate), full(w_fox_o), full(w_ca_o), full(w_o), full(lg), full(lb)],
        out_specs=(row(D), row(D)),
        scratch_shapes=[pltpu.VMEM((tm, D), BF16)],
        compiler_params=_params(2),
        name="merge_proj_ln",
    )(x, xb, ya, yb, yc, w_gate, b_gate, w_fox_o, w_ca_o, w_o, lg, lb)


def _xattn_kernel(x_ref, xb_ref, kv_ref, wq_ref, wo_ref, lg_ref, lb_ref, of_ref, ob_ref, *, alpha):
    d = x_ref.shape[1]
    dh = d // XA_HEADS
    xb = xb_ref[...]
    heads = range(XA_HEADS)
    head = lambda h: slice(h * dh, (h + 1) * dh)
    q = [(_dot(xb, wq_ref[:, head(h)]) * (dh ** -0.5 * LOG2E)).astype(BF16) for h in heads]
    s = [_dot_nt(q[h], kv_ref[:, head(h)]) for h in heads]
    m = [jnp.max(s[h], axis=-1, keepdims=True) for h in heads]
    pr = [jnp.exp2(s[h] - m[h]) for h in heads]
    l = [jnp.sum(pr[h], axis=-1, keepdims=True) for h in heads]
    outs = [(_dot(pr[h].astype(BF16), kv_ref[:, d + h * dh:d + (h + 1) * dh]) / l[h]).astype(BF16)
            for h in heads]
    hout = _dot(jnp.concatenate(outs, axis=1), wo_ref[...])
    y = _layer_norm(alpha * x_ref[...] + hout, lg_ref[...], lb_ref[...])
    of_ref[...] = y
    ob_ref[...] = y.astype(BF16)


def _cross_attention(x, xb, kv, wq, wo, lg, lb, alpha, tm):
    B, S, D = x.shape
    M = kv.shape[1]
    tm = min(tm, S)
    row = pl.BlockSpec((None, tm, D), lambda b, i: (b, i, 0))
    full = lambda a: pl.BlockSpec(a.shape, lambda b, i: (0,) * a.ndim)
    return pl.pallas_call(
        functools.partial(_xattn_kernel, alpha=alpha),
        out_shape=(jax.ShapeDtypeStruct((B, S, D), F32), jax.ShapeDtypeStruct((B, S, D), BF16)),
        grid=(B, S // tm),
        in_specs=[row, row, pl.BlockSpec((None, M, 2 * D), lambda b, i: (b, 0, 0)),
                  full(wq), full(wo), full(lg), full(lb)],
        out_specs=(row, row),
        compiler_params=_params(2),
        name="cross_attention_ln",
    )(x, xb, kv, wq, wo, lg, lb)


def _ffn_kernel(x_ref, xb_ref, wup_ref, cw_ref, cb_ref, wdn_ref, lg_ref, lb_ref, of_ref, ob_ref,
                abuf_ref, gbuf_ref, act_ref, *, alpha, d_ff):
    tm = x_ref.shape[0]
    pad = FFN_HALO_ROWS

    @pl.when(pl.program_id(1) == 0)
    def _seq_start():
        abuf_ref[0:pad, :] = jnp.zeros((pad, d_ff), F32)

    xb = xb_ref[...]
    h = jnp.zeros(x_ref.shape, F32)
    n_sub = d_ff // FFN_SUB_COLS
    for j in range(n_sub):
        cols = slice(j * FFN_SUB_COLS, (j + 1) * FFN_SUB_COLS)
        gcols = slice(d_ff + j * FFN_SUB_COLS, d_ff + (j + 1) * FFN_SUB_COLS)
        abuf_ref[pad:pad + tm, cols] = _dot(xb, wup_ref[:, cols])
        gbuf_ref[:, cols] = _dot(xb, wup_ref[:, gcols])
    for j in range(n_sub):
        cols = slice(j * FFN_SUB_COLS, (j + 1) * FFN_SUB_COLS)
        a = abuf_ref[pad:pad + tm, cols]
        a1 = abuf_ref[pad - 1:pad - 1 + tm, cols]
        a2 = abuf_ref[pad - 2:pad - 2 + tm, cols]
        conv = (cw_ref[0:1, cols] * a2 + cw_ref[1:2, cols] * a1 + cw_ref[2:3, cols] * a
                + cb_ref[:, cols])
        act_ref[:, cols] = (jax.nn.gelu(conv) * gbuf_ref[:, cols]).astype(BF16)
        if (j + 1) % FFN_DOWN_GROUP == 0 or j == n_sub - 1:
            lo = (j // FFN_DOWN_GROUP) * FFN_DOWN_GROUP * FFN_SUB_COLS
            rows = slice(lo, (j + 1) * FFN_SUB_COLS)
            h = h + _dot(act_ref[:, rows], wdn_ref[rows, :])
    abuf_ref[0:pad, :] = abuf_ref[tm:tm + pad, :]
    y = _layer_norm(alpha * x_ref[...] + h, lg_ref[...], lb_ref[...])
    of_ref[...] = y
    ob_ref[...] = y.astype(BF16)


def _conv_ffn(x, xb, w_up, conv_w, conv_b, w_down, lg, lb, alpha, tm):
    B, S, D = x.shape
    d_ff = w_down.shape[0]
    tm = min(tm, S)
    row = pl.BlockSpec((None, tm, D), lambda b, i: (b, i, 0))
    full = lambda a: pl.BlockSpec(a.shape, lambda b, i: (0,) * a.ndim)
    once = lambda a: pl.BlockSpec(a.shape, lambda b, i: (0,) * a.ndim, pipeline_mode=pl.Buffered(1))
    return pl.pallas_call(
        functools.partial(_ffn_kernel, alpha=alpha, d_ff=d_ff),
        out_shape=(jax.ShapeDtypeStruct((B, S, D), F32), jax.ShapeDtypeStruct((B, S, D), BF16)),
        grid=(B, S // tm),
        in_specs=[row, row, once(w_up), full(conv_w), full(conv_b), once(w_down), full(lg), full(lb)],
        out_specs=(row, row),
        scratch_shapes=[pltpu.VMEM((tm + FFN_HALO_ROWS, d_ff), F32), pltpu.VMEM((tm, d_ff), F32),
                        pltpu.VMEM((tm, d_ff), BF16)],
        compiler_params=_params(2),
        name="conv_ffn_ln",
    )(x, xb, w_up, conv_w, conv_b, w_down, lg, lb)


def _split_in_proj(w_in, b_in, d_model):
    o = np.cumsum([0, FOX_WIDTH, FOX_WIDTH, FOX_WIDTH, FOX_HEADS, SSM_WIDTH,
                   CA_WIDTH, CA_WIDTH, CA_WIDTH, 3 * d_model])
    fqkv, ff, su, cqkv, gates = (slice(o[0], o[3]), slice(o[3], o[4]), slice(o[4], o[5]),
                                 slice(o[5], o[8]), slice(o[8], o[9]))
    cols = lambda s: (w_in[:, s], b_in[s])
    w_qkv = jnp.concatenate([w_in[:, fqkv], w_in[:, cqkv]], axis=1)
    b_qkv = jnp.concatenate([b_in[fqkv], b_in[cqkv]])
    w_ff, b_ff = cols(ff)
    pad = LANES - FOX_HEADS
    w_ff = jnp.pad(w_ff, ((0, 0), (0, pad)))
    b_ff = jnp.pad(b_ff, (0, pad))
    prep = lambda w, b: (w.astype(BF16), b.astype(F32)[None, :])
    return prep(w_qkv, b_qkv), prep(*cols(gates)), prep(*cols(su)), prep(w_ff, b_ff)


def kernel(x, mem, w_in, b_in, ssm_lambda_re, ssm_lambda_im, ssm_log_dt, ssm_b_re, ssm_b_im, ssm_c_re, ssm_c_im, ssm_d, ca_rel_bias, w_fox_o, w_ssm_glu, w_ca_o, w_o, xa_wq, xa_wkv, xa_wo, ffn_w_up, ffn_conv_w, ffn_conv_b, ffn_w_down, ln_g, ln_b):
    B, S, D = x.shape
    depth = w_in.shape[0]
    d_ff = ffn_w_down.shape[1]
    alpha = float((2 * depth) ** 0.25)
    mem_b = mem.astype(BF16)
    xf = x.astype(F32)
    xb = x.astype(BF16)
    zero_bias = jnp.zeros((1, 2 * D), F32)
    for l in range(depth):
        (w_qkv, b_qkv), (w_gate, b_gate), (w_su, b_su), (w_ff, b_ff) = _split_in_proj(w_in[l], b_in[l], D)
        z = _matmul(xb, w_qkv, b_qkv, BF16, tm=512, name="in_proj")
        qfeat, kfeat = _forget_decay_features(xb, w_ff, b_ff)
        ya = _fox_attention(z, qfeat, kfeat, 0)
        yc = _chunk_attention(z, _ca_bias_vector(ca_rel_bias[l], ATT_BLOCK), 3 * FOX_WIDTH // LANES)
        bbr, bbi, cc, ar, ai = _s5_params(ssm_lambda_re[l], ssm_lambda_im[l], ssm_log_dt[l],
                                          ssm_b_re[l], ssm_b_im[l], ssm_c_re[l], ssm_c_im[l], B)
        yb = _s5_branch(xb, w_su, b_su, bbr, bbi, cc, ar, ai,
                        ssm_d[l].astype(F32)[None, :], w_ssm_glu[l].astype(BF16))
        ln = lambda k: (ln_g[l, k].astype(F32)[None, :], ln_b[l, k].astype(F32)[None, :])
        xf, xb = _merge(xf, xb, ya, yb, yc, w_gate, b_gate, w_fox_o[l].astype(BF16),
                        w_ca_o[l].astype(BF16), w_o[l].astype(BF16), *ln(0), alpha, tm=512)
        kv = _matmul(mem_b, xa_wkv[l].astype(BF16), zero_bias, BF16, tm=256, name="mem_kv")
        xf, xb = _cross_attention(xf, xb, kv, xa_wq[l].astype(BF16), xa_wo[l].astype(BF16),
                                  *ln(1), alpha, tm=512)
        xf, xb = _conv_ffn(xf, xb, ffn_w_up[l].astype(BF16), ffn_conv_w[l].astype(F32),
                           ffn_conv_b[l].astype(F32)[None, :], ffn_w_down[l].astype(BF16),
                           *ln(2), alpha, tm=512)
    return xf.astype(x.dtype)
```

```python
import functools
import math

import jax
import jax.numpy as jnp
import numpy as np
from jax import lax
from jax.experimental import pallas as pl
from jax.experimental.pallas import tpu as pltpu

F32 = jnp.float32
BF16 = jnp.bfloat16

CHUNK = 64
Q_BLOCK = 128
FOX_HEADS = 8
FOX_HEAD_DIM = 64
FOX_WIDTH = FOX_HEADS * FOX_HEAD_DIM
SSM_GROUP = 16
SSM_WIDTH = 512
SSM_GROUPS = SSM_WIDTH // SSM_GROUP
SSM_STATE = 64
CA_HEADS = 8
CA_HEAD_DIM = 64
CA_WIDTH = CA_HEADS * CA_HEAD_DIM
CA_LEFT_CHUNKS = 8
REL_MIN = -(CHUNK - 1)
REL_MAX = 4 * CHUNK
XA_HEADS = 4
CONV_WIDTH = 3
LN_EPS = 1e-5
NEG_INF = -1e30

LOG2E = math.log2(math.e)
DECAY_LANES = 6
DECAY_STRIDE = 8
LANES = 128
HEAD_PAIR = LANES // FOX_HEAD_DIM
ATT_BLOCK = 256
FOX_BLOCK = 512
CA_WINDOW_BLOCKS = 3
SSM_SLAB_GROUPS = LANES // SSM_GROUP
SSM_SLABS = SSM_GROUPS // SSM_SLAB_GROUPS
SSM_SLAB_STATE = SSM_SLAB_GROUPS * SSM_STATE
SSM_TIME_BLOCK = 128
MATMUL_SUB_COLS = 512
MERGE_SUB_COLS = 256
FFN_SUB_COLS = 256
FFN_DOWN_GROUP = 4
FFN_HALO_ROWS = 8
VMEM_LIMIT = 56 * 1024 * 1024

_ARB = "arbitrary"


def _params(n_axes):
    return pltpu.CompilerParams(dimension_semantics=(_ARB,) * n_axes,
                                vmem_limit_bytes=VMEM_LIMIT)


def _whole(a):
    return pl.BlockSpec(a.shape, lambda *_: (0,) * a.ndim)


def _layer(a, l, **kwargs):
    return pl.BlockSpec((None,) + a.shape[1:], lambda *_: (l,) + (0,) * (a.ndim - 1), **kwargs)


def _layer_norm(v, g, b):
    mu = jnp.mean(v, axis=-1, keepdims=True)
    d = v - mu
    var = jnp.mean(d * d, axis=-1, keepdims=True)
    return d * lax.rsqrt(var + LN_EPS) * g + b


def _dot(a, b):
    return jnp.dot(a, b, preferred_element_type=F32)


def _dot_nt(a, b):
    return lax.dot_general(a, b, (((1,), (1,)), ((), ())), preferred_element_type=F32)


def _mm_kernel(x_ref, w_ref, b_ref, o_ref):
    x = x_ref[...]
    n = w_ref.shape[1]
    sub = min(MATMUL_SUB_COLS, n)
    for c in range(n // sub):
        cols = slice(c * sub, (c + 1) * sub)
        o_ref[:, cols] = (_dot(x, w_ref[:, cols]) + b_ref[:, cols]).astype(o_ref.dtype)


def _matmul(x, w, l, bias, out_dtype, *, tm, name="matmul"):
    B, S, K = x.shape
    N = w.shape[2]
    tm = min(tm, S)
    return pl.pallas_call(
        _mm_kernel,
        out_shape=jax.ShapeDtypeStruct((B, S, N), out_dtype),
        grid=(B, S // tm),
        in_specs=[pl.BlockSpec((None, tm, K), lambda b, i: (b, i, 0)), _layer(w, l), _whole(bias)],
        out_specs=pl.BlockSpec((None, tm, N), lambda b, i: (b, i, 0)),
        compiler_params=_params(2),
        name=name,
    )(x, w, bias)


def _split3(c):
    hi = c.astype(BF16).astype(F32)
    r = c - hi
    mid = r.astype(BF16).astype(F32)
    lo = (r - mid).astype(BF16).astype(F32)
    return hi, mid, lo


def _decay_scatter(query_side):
    r = lax.broadcasted_iota(jnp.int32, (3 * LANES, LANES), 0)
    c = lax.broadcasted_iota(jnp.int32, (3 * LANES, LANES), 1)
    piece, h = r // LANES, r % LANES
    dst = DECAY_STRIDE * h + piece + (0 if query_side else 3)
    mat = jnp.where((h < FOX_HEADS) & (c == dst), 1.0 if query_side else -1.0, 0.0).astype(BF16)
    lane = lax.broadcasted_iota(jnp.int32, (1, LANES), 1)
    k = lane % DECAY_STRIDE
    ones_lo = 3 if query_side else 0
    ones = jnp.where((lane < DECAY_STRIDE * FOX_HEADS) & (k >= ones_lo) & (k < ones_lo + 3), 1.0, 0.0)
    return mat, ones


def _cum_kernel(x_ref, w_ref, b_ref, qf_ref, kf_ref, *, blk):
    S = x_ref.shape[0]
    row = lax.broadcasted_iota(jnp.int32, (blk, blk), 0)
    col = lax.broadcasted_iota(jnp.int32, (blk, blk), 1)
    tri = jnp.where(row >= col, 1.0, 0.0).astype(BF16)
    q_mat, q_ones = _decay_scatter(True)
    k_mat, k_ones = _decay_scatter(False)
    local = []
    for i in range(S // blk):
        x = _dot(x_ref[i * blk:(i + 1) * blk, :], w_ref[...]) + b_ref[...]
        ls = jnp.minimum(x, 0.0) - jnp.log(1.0 + jnp.exp(-jnp.abs(x)))
        hi, mid, lo = _split3(ls)
        local.append(_dot(tri, hi.astype(BF16)) + _dot(tri, mid.astype(BF16))
                     + _dot(tri, lo.astype(BF16)))
    carry = jnp.zeros((1, LANES), F32)
    for i in range(S // blk):
        cs = local[i] + carry
        carry = cs[blk - 1:blk, :]
        pieces = jnp.concatenate([v.astype(BF16) for v in _split3(cs * LOG2E)], axis=1)
        qf_ref[i * blk:(i + 1) * blk, :] = _dot(pieces, q_mat) + q_ones
        kf_ref[i * blk:(i + 1) * blk, :] = _dot(pieces, k_mat) + k_ones


def _forget_decay_features(xb, w_ff, l, b_ff):
    B, S, D = xb.shape
    L = w_ff.shape[2]
    out = jax.ShapeDtypeStruct((B, S, L), F32)
    spec = pl.BlockSpec((None, S, L), lambda b: (b, 0, 0))
    return pl.pallas_call(
        functools.partial(_cum_kernel, blk=min(ATT_BLOCK, S)),
        out_shape=(out, out),
        grid=(B,),
        in_specs=[pl.BlockSpec((None, S, D), lambda b: (b, 0, 0)), _layer(w_ff, l), _whole(b_ff)],
        out_specs=(spec, spec),
        compiler_params=_params(1),
        name="forget_cumsum",
    )(xb, w_ff, b_ff)


def _head_lane_mask(j, shape):
    lane = lax.broadcasted_iota(jnp.int32, shape, len(shape) - 1)
    lo = j * FOX_HEAD_DIM
    return (lane >= lo) & (lane < lo + FOX_HEAD_DIM)


def _with_decay_lanes(x, feat, j, p):
    a0 = (1 - j) * FOX_HEAD_DIM
    shift = a0 - DECAY_STRIDE * j + (LANES if j else 0) - DECAY_STRIDE * HEAD_PAIR * p
    lane = lax.broadcasted_iota(jnp.int32, x.shape, 1)
    window = (lane >= a0) & (lane < a0 + DECAY_LANES)
    aug = jnp.where(window, pltpu.roll(feat, shift, 1), 0.0)
    return jnp.where(_head_lane_mask(j, x.shape), x, aug).astype(BF16)


def _fox_kernel(q_ref, k_ref, v_ref, qf_ref, kf_ref, o_ref, kaug_ref, vaug_ref, s_ref, mb_ref, *, blk):
    p = pl.program_id(1)
    S = k_ref.shape[0]
    n_blk = S // blk
    tri = lambda qi, c: qi * (qi + 1) // 2 + c

    for r in range(n_blk):
        rows = slice(r * blk, (r + 1) * blk)
        kf = k_ref[rows, :].astype(F32)
        vf = v_ref[rows, :].astype(F32)
        feat = kf_ref[rows, :]
        for j in range(HEAD_PAIR):
            kaug_ref[j, rows, :] = _with_decay_lanes(kf, feat, j, p)
            vaug_ref[j, rows, :] = jnp.where(_head_lane_mask(j, vf.shape), vf, 1.0).astype(BF16)

    for qi in range(n_blk):
        rows = slice(qi * blk, (qi + 1) * blk)
        qf = q_ref[rows, :].astype(F32) * (FOX_HEAD_DIM ** -0.5 * LOG2E)
        feat = qf_ref[rows, :]
        for j in range(HEAD_PAIR):
            qaug = _with_decay_lanes(qf, feat, j, p)
            ml = None
            for c in range(qi + 1):
                s = _dot_nt(qaug, kaug_ref[j, c * blk:(c + 1) * blk, :])
                if c == qi:
                    row = lax.broadcasted_iota(jnp.int32, (blk, blk), 0)
                    colk = lax.broadcasted_iota(jnp.int32, (blk, blk), 1)
                    s = jnp.where(row >= colk, s, NEG_INF)
                s_ref[tri(qi, c), j] = s
                for g in range(blk // LANES):
                    sg = s[:, g * LANES:(g + 1) * LANES]
                    ml = sg if ml is None else jnp.maximum(ml, sg)
            mb_ref[qi, j] = jnp.broadcast_to(jnp.max(ml, axis=1, keepdims=True), (blk, LANES))

    for qi in range(n_blk):
        outs = []
        for j in range(HEAD_PAIR):
            mb = jnp.concatenate([mb_ref[qi, j]] * (blk // LANES), axis=1)
            acc = None
            for c in range(qi + 1):
                pr = jnp.exp2((s_ref[tri(qi, c), j] - mb).astype(BF16))
                d = _dot(pr, vaug_ref[j, c * blk:(c + 1) * blk, :])
                acc = d if acc is None else acc + d
            outs.append(acc / pltpu.roll(acc, FOX_HEAD_DIM, 1))
        o_ref[qi * blk:(qi + 1) * blk, :] = jnp.where(
            _head_lane_mask(0, outs[0].shape), outs[0], outs[1]).astype(o_ref.dtype)


def _fox_attention(z, qfeat, kfeat, base):
    B, S, _ = z.shape
    blk = min(FOX_BLOCK, S)
    n_q = S // blk
    n_pair = FOX_WIDTH // LANES
    return pl.pallas_call(
        functools.partial(_fox_kernel, blk=blk),
        out_shape=jax.ShapeDtypeStruct((B, S, FOX_WIDTH), BF16),
        grid=(B, n_pair),
        in_specs=[pl.BlockSpec((None, S, LANES), lambda b, p: (b, 0, base + p)),
                  pl.BlockSpec((None, S, LANES), lambda b, p: (b, 0, base + n_pair + p)),
                  pl.BlockSpec((None, S, LANES), lambda b, p: (b, 0, base + 2 * n_pair + p)),
                  pl.BlockSpec((None, S, LANES), lambda b, p: (b, 0, 0)),
                  pl.BlockSpec((None, S, LANES), lambda b, p: (b, 0, 0))],
        out_specs=pl.BlockSpec((None, S, LANES), lambda b, p: (b, 0, p)),
        scratch_shapes=[pltpu.VMEM((HEAD_PAIR, S, LANES), BF16),
                        pltpu.VMEM((HEAD_PAIR, S, LANES), BF16),
                        pltpu.VMEM((n_q * (n_q + 1) // 2, HEAD_PAIR, blk, blk), F32),
                        pltpu.VMEM((n_q, HEAD_PAIR, blk, LANES), F32)],
        compiler_params=_params(2),
        name="fox_attention",
    )(z, z, z, qfeat, kfeat)


def _ca_kernel(q_ref, k_ref, v_ref, vec_ref, o_ref, u_ref, vaug_ref, s_ref, mb_ref, *, blk):
    S = q_ref.shape[0]
    left = CA_WINDOW_BLOCKS - 1
    n_blk = S // blk
    kblocks = [[(j - left + i, i) for i in range(CA_WINDOW_BLOCKS) if j - left + i >= 0]
               for j in range(n_blk)]

    @pl.when(pl.program_id(1) == 0)
    def _build_table():
        row_chunk = lax.broadcasted_iota(jnp.int32, (blk, blk), 0) // CHUNK
        col_chunk = lax.broadcasted_iota(jnp.int32, (blk, blk), 1) // CHUNK
        for hh in range(HEAD_PAIR):
            vec = jnp.broadcast_to(vec_ref[hh:hh + 1, :], (blk, vec_ref.shape[1]))
            toep = pltpu.roll(vec, 0, 1, stride=1, stride_axis=0)
            for i in range(CA_WINDOW_BLOCKS):
                dchunk = row_chunk - col_chunk + (left - i) * (blk // CHUNK)
                valid = (dchunk >= 0) & (dchunk <= CA_LEFT_CHUNKS)
                u_ref[hh, i] = jnp.where(valid, toep[:, i * blk:(i + 1) * blk], NEG_INF)

    for r in range(n_blk):
        rows = slice(r * blk, (r + 1) * blk)
        vf = v_ref[rows, :].astype(F32)
        for hh in range(HEAD_PAIR):
            vaug_ref[hh, rows, :] = jnp.where(_head_lane_mask(hh, vf.shape), vf, 1.0).astype(BF16)

    for j in range(n_blk):
        qf = q_ref[j * blk:(j + 1) * blk, :].astype(F32) * (CA_HEAD_DIM ** -0.5 * LOG2E)
        for hh in range(HEAD_PAIR):
            qm = jnp.where(_head_lane_mask(hh, qf.shape), qf, 0.0).astype(BF16)
            ml = None
            for kb, i in kblocks[j]:
                s = _dot_nt(qm, k_ref[kb * blk:(kb + 1) * blk, :]) + u_ref[hh, i]
                s_ref[j, hh, i] = s
                for g in range(blk // LANES):
                    sg = s[:, g * LANES:(g + 1) * LANES]
                    ml = sg if ml is None else jnp.maximum(ml, sg)
            mb_ref[j, hh] = jnp.broadcast_to(jnp.max(ml, axis=1, keepdims=True), (blk, LANES))

    for j in range(n_blk):
        outs = []
        for hh in range(HEAD_PAIR):
            mb = jnp.concatenate([mb_ref[j, hh]] * (blk // LANES), axis=1)
            acc = None
            for kb, i in kblocks[j]:
                pr = jnp.exp2((s_ref[j, hh, i] - mb).astype(BF16))
                d = _dot(pr, vaug_ref[hh, kb * blk:(kb + 1) * blk, :])
                acc = d if acc is None else acc + d
            outs.append(acc / pltpu.roll(acc, CA_HEAD_DIM, 1))
        o_ref[j * blk:(j + 1) * blk, :] = jnp.where(
            _head_lane_mask(0, outs[0].shape), outs[0], outs[1]).astype(o_ref.dtype)


def _ca_bias_vector(rel_bias, blk):
    n_m = CA_WINDOW_BLOCKS * blk
    shift = (CA_WINDOW_BLOCKS - 1) * blk
    period = n_m + blk
    x = np.arange(period)
    x = np.where(x >= n_m, x - period, x)
    vec = rel_bias.astype(F32)[:, np.clip(shift - x, REL_MIN, REL_MAX) - REL_MIN] * LOG2E
    return vec.reshape(rel_bias.shape[0] // HEAD_PAIR, HEAD_PAIR, period)


def _chunk_attention(z, bias_vec, base):
    B, S, _ = z.shape
    blk = ATT_BLOCK
    n_pair = CA_WIDTH // LANES
    period = bias_vec.shape[2]
    return pl.pallas_call(
        functools.partial(_ca_kernel, blk=blk),
        out_shape=jax.ShapeDtypeStruct((B, S, CA_WIDTH), BF16),
        grid=(n_pair, B),
        in_specs=[pl.BlockSpec((None, S, LANES), lambda p, b: (b, 0, base + p)),
                  pl.BlockSpec((None, S, LANES), lambda p, b: (b, 0, base + n_pair + p)),
                  pl.BlockSpec((None, S, LANES), lambda p, b: (b, 0, base + 2 * n_pair + p)),
                  pl.BlockSpec((None, HEAD_PAIR, period), lambda p, b: (p, 0, 0))],
        out_specs=pl.BlockSpec((None, S, LANES), lambda p, b: (b, 0, p)),
        scratch_shapes=[pltpu.VMEM((HEAD_PAIR, CA_WINDOW_BLOCKS, blk, blk), F32),
                        pltpu.VMEM((HEAD_PAIR, S, LANES), BF16),
                        pltpu.VMEM((S // blk, HEAD_PAIR, CA_WINDOW_BLOCKS, blk, blk), F32),
                        pltpu.VMEM((S // blk, HEAD_PAIR, blk, LANES), F32)],
        compiler_params=_params(2),
        name="chunk_attention",
    )(z, z, z, bias_vec)


def _s5_kernel(x_ref, wsu_ref, bsu_ref, bbr_ref, bbi_ref, cc_ref, ar_ref, ai_ref, d_ref, wg_ref, o_ref,
               ut_ref, xr_ref, xi_ref, yt_ref, hr_ref, hi_ref, *, steps, batch):
    @pl.when(pl.program_id(0) == 0)
    def _init():
        hr_ref[...] = jnp.zeros_like(hr_ref)
        hi_ref[...] = jnp.zeros_like(hi_ref)

    for b in range(batch):
        su = _dot(x_ref[b], wsu_ref[...]) + bsu_ref[...]
        for i in range(SSM_SLABS):
            ut_ref[i, pl.ds(b, steps, stride=batch), :] = su[:, i * LANES:(i + 1) * LANES]

    slab = lambda i: slice(i * SSM_SLAB_STATE, (i + 1) * SSM_SLAB_STATE)
    for i in range(SSM_SLABS):
        ui = ut_ref[i].astype(BF16)
        xr_ref[:, slab(i)] = _dot(ui, bbr_ref[i])
        xi_ref[:, slab(i)] = _dot(ui, bbi_ref[i])
    for i in range(SSM_SLABS):
        cols = slab(i)
        ar = ar_ref[:, cols]
        ai = ai_ref[:, cols]
        hr = hr_ref[:, cols]
        hi = hi_ref[:, cols]
        for t in range(steps):
            rows = slice(t * batch, (t + 1) * batch)
            hr, hi = (ar * hr - ai * hi + xr_ref[rows, cols], ar * hi + ai * hr + xi_ref[rows, cols])
            xr_ref[rows, cols] = hr
            xi_ref[rows, cols] = hi
        hr_ref[:, cols] = hr
        hi_ref[:, cols] = hi
    ys = []
    for i in range(SSM_SLABS):
        cols = slab(i)
        h = jnp.concatenate([xr_ref[:, cols].astype(BF16), xi_ref[:, cols].astype(BF16)], axis=1)
        ys.append(_dot(h, cc_ref[i]) + d_ref[:, i * LANES:(i + 1) * LANES] * ut_ref[i])
    g = jax.nn.gelu(jnp.concatenate(ys, axis=1)).astype(BF16)
    half = wg_ref.shape[1] // 2
    for c in range(half // MERGE_SUB_COLS):
        cols = slice(c * MERGE_SUB_COLS, (c + 1) * MERGE_SUB_COLS)
        gcols = slice(half + c * MERGE_SUB_COLS, half + (c + 1) * MERGE_SUB_COLS)
        out = _dot(g, wg_ref[:, cols]) * jax.nn.sigmoid(_dot(g, wg_ref[:, gcols]))
        for k in range(MERGE_SUB_COLS // LANES):
            yt_ref[c * (MERGE_SUB_COLS // LANES) + k] = out[:, k * LANES:(k + 1) * LANES]
    for b in range(batch):
        for k in range(half // LANES):
            o_ref[b, :, k * LANES:(k + 1) * LANES] = (
                yt_ref[k, pl.ds(b, steps, stride=batch), :].astype(o_ref.dtype))


def _s5_branch(xb, l, w_su, b_su, bbr, bbi, cc, ar, ai, d, w_glu):
    B, S, D = xb.shape
    steps = min(SSM_TIME_BLOCK, S)
    tr = steps * B
    n_state = SSM_GROUPS * SSM_STATE
    d_out = w_glu.shape[2] // 2
    return pl.pallas_call(
        functools.partial(_s5_kernel, steps=steps, batch=B),
        out_shape=jax.ShapeDtypeStruct((B, S, d_out), BF16),
        grid=(S // steps,),
        in_specs=[pl.BlockSpec((B, steps, D), lambda t: (0, t, 0)),
                  _layer(w_su, l), _whole(b_su), _whole(bbr), _whole(bbi), _whole(cc), _whole(ar),
                  _whole(ai), _whole(d), _layer(w_glu, l)],
        out_specs=pl.BlockSpec((B, steps, d_out), lambda t: (0, t, 0)),
        scratch_shapes=[pltpu.VMEM((SSM_SLABS, tr, LANES), F32),
                        pltpu.VMEM((tr, n_state), F32), pltpu.VMEM((tr, n_state), F32),
                        pltpu.VMEM((d_out // LANES, tr, LANES), F32),
                        pltpu.VMEM((B, n_state), F32), pltpu.VMEM((B, n_state), F32)],
        compiler_params=_params(1),
        name="s5_branch",
    )(xb, w_su, b_su, bbr, bbi, cc, ar, ai, d, w_glu)


def _s5_params(lam_re, lam_im, log_dt, b_re, b_im, c_re, c_im, batch):
    lr = jnp.minimum(lam_re.astype(F32), -1e-4)
    li = lam_im.astype(F32)
    dt = jnp.exp(log_dt.astype(F32))[:, None]
    mag = jnp.exp(lr * dt)
    ar = mag * jnp.cos(li * dt)
    ai = mag * jnp.sin(li * dt)
    den = lr * lr + li * li
    gr = ((ar - 1.0) * lr + ai * li) / den
    gi = (ai * lr - (ar - 1.0) * li) / den
    br, bi = b_re.astype(F32), b_im.astype(F32)
    bbr = gr[..., None] * br - gi[..., None] * bi
    bbi = gr[..., None] * bi + gi[..., None] * br
    eye = jnp.eye(SSM_SLAB_GROUPS, dtype=F32)

    def in_map(bb):
        bb = bb.reshape(SSM_SLABS, SSM_SLAB_GROUPS, SSM_STATE, SSM_GROUP)
        return jnp.einsum('sgpc,gh->sgchp', bb, eye).reshape(SSM_SLABS, LANES, SSM_SLAB_STATE).astype(BF16)

    def out_map(c):
        c = c.astype(F32).reshape(SSM_SLABS, SSM_SLAB_GROUPS, SSM_GROUP, SSM_STATE)
        return jnp.einsum('sgcp,gh->sgphc', c, eye).reshape(SSM_SLABS, SSM_SLAB_STATE, LANES)

    cc = jnp.concatenate([out_map(c_re), -out_map(c_im)], axis=1).astype(BF16)
    n_state = SSM_GROUPS * SSM_STATE
    ar_b = jnp.broadcast_to(ar.reshape(1, n_state), (batch, n_state))
    ai_b = jnp.broadcast_to(ai.reshape(1, n_state), (batch, n_state))
    return in_map(bbr), in_map(bbi), cc, ar_b, ai_b


def _merge_kernel(x_ref, xb_ref, ya_ref, yb_ref, yc_ref, wg_ref, bg_ref, wa_ref, wc_ref, wo_ref,
                  lg_ref, lb_ref, of_ref, ob_ref, m_ref, *, alpha):
    d = x_ref.shape[1]
    xb = xb_ref[...]
    ya = ya_ref[...]
    yc = yc_ref[...]
    for c in range(d // MERGE_SUB_COLS):
        cols = slice(c * MERGE_SUB_COLS, (c + 1) * MERGE_SUB_COLS)

        def gate(k, cols=cols):
            gcols = slice(k * d + cols.start, k * d + cols.stop)
            return jax.nn.sigmoid(_dot(xb, wg_ref[:, gcols]) + bg_ref[:, gcols])

        merged = (gate(0) * _dot(ya, wa_ref[:, cols])
                  + gate(1) * yb_ref[:, cols].astype(F32)
                  + gate(2) * _dot(yc, wc_ref[:, cols]))
        m_ref[:, cols] = merged.astype(BF16)
    h = _dot(m_ref[...], wo_ref[...])
    y = _layer_norm(alpha * x_ref[...] + h, lg_ref[...], lb_ref[...])
    of_ref[...] = y
    ob_ref[...] = y.astype(BF16)


def _merge(x, xb, ya, yb, yc, l, w_gate, b_gate, w_fox_o, w_ca_o, w_o, lg, lb, alpha, tm):
    B, S, D = x.shape
    tm = min(tm, S)
    row = lambda w: pl.BlockSpec((None, tm, w), lambda b, i: (b, i, 0))
    return pl.pallas_call(
        functools.partial(_merge_kernel, alpha=alpha),
        out_shape=(jax.ShapeDtypeStruct((B, S, D), F32), jax.ShapeDtypeStruct((B, S, D), BF16)),
        grid=(B, S // tm),
        in_specs=[row(D), row(D), row(ya.shape[2]), row(D), row(yc.shape[2]),
                  _layer(w_gate, l), _whole(b_gate), _layer(w_fox_o, l), _layer(w_ca_o, l),
                  _layer(w_o, l), _whole(lg), _whole(lb)],
        out_specs=(row(D), row(D)),
        scratch_shapes=[pltpu.VMEM((tm, D), BF16)],
        compiler_params=_params(2),
        name="merge_proj_ln",
    )(x, xb, ya, yb, yc, w_gate, b_gate, w_fox_o, w_ca_o, w_o, lg, lb)


def _xattn_kernel(x_ref, xb_ref, kv_ref, wq_ref, wo_ref, lg_ref, lb_ref, of_ref, ob_ref, *, alpha):
    d = x_ref.shape[1]
    dh = d // XA_HEADS
    xb = xb_ref[...]
    heads = range(XA_HEADS)
    head = lambda h: slice(h * dh, (h + 1) * dh)
    q = [(_dot(xb, wq_ref[:, head(h)]) * (dh ** -0.5 * LOG2E)).astype(BF16) for h in heads]
    s = [_dot_nt(q[h], kv_ref[:, head(h)]) for h in heads]
    m = [jnp.max(s[h], axis=-1, keepdims=True) for h in heads]
    pr = [jnp.exp2(s[h] - m[h]) for h in heads]
    l = [jnp.sum(pr[h], axis=-1, keepdims=True) for h in heads]
    outs = [(_dot(pr[h].astype(BF16), kv_ref[:, d + h * dh:d + (h + 1) * dh]) / l[h]).astype(BF16)
            for h in heads]
    hout = _dot(jnp.concatenate(outs, axis=1), wo_ref[...])
    y = _layer_norm(alpha * x_ref[...] + hout, lg_ref[...], lb_ref[...])
    of_ref[...] = y
    ob_ref[...] = y.astype(BF16)


def _cross_attention(x, xb, kv, l, wq, wo, lg, lb, alpha, tm):
    B, S, D = x.shape
    M = kv.shape[1]
    tm = min(tm, S)
    row = pl.BlockSpec((None, tm, D), lambda b, i: (b, i, 0))
    return pl.pallas_call(
        functools.partial(_xattn_kernel, alpha=alpha),
        out_shape=(jax.ShapeDtypeStruct((B, S, D), F32), jax.ShapeDtypeStruct((B, S, D), BF16)),
        grid=(B, S // tm),
        in_specs=[row, row, pl.BlockSpec((None, M, 2 * D), lambda b, i: (b, 0, 0)),
                  _layer(wq, l), _layer(wo, l), _whole(lg), _whole(lb)],
        out_specs=(row, row),
        compiler_params=_params(2),
        name="cross_attention_ln",
    )(x, xb, kv, wq, wo, lg, lb)


def _ffn_kernel(x_ref, xb_ref, wup_ref, cw_ref, cb_ref, wdn_ref, lg_ref, lb_ref, of_ref, ob_ref,
                abuf_ref, gbuf_ref, act_ref, *, alpha, d_ff):
    tm = x_ref.shape[0]
    pad = FFN_HALO_ROWS

    @pl.when(pl.program_id(1) == 0)
    def _seq_start():
        abuf_ref[0:pad, :] = jnp.zeros((pad, d_ff), F32)

    xb = xb_ref[...]
    h = jnp.zeros(x_ref.shape, F32)
    n_sub = d_ff // FFN_SUB_COLS
    for j in range(n_sub):
        cols = slice(j * FFN_SUB_COLS, (j + 1) * FFN_SUB_COLS)
        gcols = slice(d_ff + j * FFN_SUB_COLS, d_ff + (j + 1) * FFN_SUB_COLS)
        abuf_ref[pad:pad + tm, cols] = _dot(xb, wup_ref[:, cols])
        gbuf_ref[:, cols] = _dot(xb, wup_ref[:, gcols])
    for j in range(n_sub):
        cols = slice(j * FFN_SUB_COLS, (j + 1) * FFN_SUB_COLS)
        a = abuf_ref[pad:pad + tm, cols]
        a1 = abuf_ref[pad - 1:pad - 1 + tm, cols]
        a2 = abuf_ref[pad - 2:pad - 2 + tm, cols]
        conv = (cw_ref[0:1, cols] * a2 + cw_ref[1:2, cols] * a1 + cw_ref[2:3, cols] * a
                + cb_ref[:, cols])
        act_ref[:, cols] = (jax.nn.gelu(conv) * gbuf_ref[:, cols]).astype(BF16)
        if (j + 1) % FFN_DOWN_GROUP == 0 or j == n_sub - 1:
            lo = (j // FFN_DOWN_GROUP) * FFN_DOWN_GROUP * FFN_SUB_COLS
            rows = slice(lo, (j + 1) * FFN_SUB_COLS)
            h = h + _dot(act_ref[:, rows], wdn_ref[rows, :])
    abuf_ref[0:pad, :] = abuf_ref[tm:tm + pad, :]
    y = _layer_norm(alpha * x_ref[...] + h, lg_ref[...], lb_ref[...])
    of_ref[...] = y
    ob_ref[...] = y.astype(BF16)


def _conv_ffn(x, xb, l, w_up, conv_w, conv_b, w_down, lg, lb, alpha, tm):
    B, S, D = x.shape
    d_ff = w_down.shape[1]
    tm = min(tm, S)
    row = pl.BlockSpec((None, tm, D), lambda b, i: (b, i, 0))
    once = pl.Buffered(1)
    return pl.pallas_call(
        functools.partial(_ffn_kernel, alpha=alpha, d_ff=d_ff),
        out_shape=(jax.ShapeDtypeStruct((B, S, D), F32), jax.ShapeDtypeStruct((B, S, D), BF16)),
        grid=(B, S // tm),
        in_specs=[row, row, _layer(w_up, l, pipeline_mode=once), _whole(conv_w), _whole(conv_b),
                  _layer(w_down, l, pipeline_mode=once), _whole(lg), _whole(lb)],
        out_specs=(row, row),
        scratch_shapes=[pltpu.VMEM((tm + FFN_HALO_ROWS, d_ff), F32), pltpu.VMEM((tm, d_ff), F32),
                        pltpu.VMEM((tm, d_ff), BF16)],
        compiler_params=_params(2),
        name="conv_ffn_ln",
    )(x, xb, w_up, conv_w, conv_b, w_down, lg, lb)


def _split_in_proj(w_in, b_in, d_model):
    o = np.cumsum([0, FOX_WIDTH, FOX_WIDTH, FOX_WIDTH, FOX_HEADS, SSM_WIDTH,
                   CA_WIDTH, CA_WIDTH, CA_WIDTH, 3 * d_model])
    fqkv, ff, su, cqkv, gates = (slice(o[0], o[3]), slice(o[3], o[4]), slice(o[4], o[5]),
                                 slice(o[5], o[8]), slice(o[8], o[9]))
    cols = lambda s: (w_in[..., s], b_in[..., s])
    w_qkv = jnp.concatenate([w_in[..., fqkv], w_in[..., cqkv]], axis=-1)
    b_qkv = jnp.concatenate([b_in[..., fqkv], b_in[..., cqkv]], axis=-1)
    w_ff, b_ff = cols(ff)
    pad = LANES - FOX_HEADS
    w_ff = jnp.pad(w_ff, ((0, 0), (0, 0), (0, pad)))
    b_ff = jnp.pad(b_ff, ((0, 0), (0, pad)))
    prep = lambda w, b: (w.astype(BF16), b.astype(F32)[:, None, :])
    return prep(w_qkv, b_qkv), prep(*cols(gates)), prep(*cols(su)), prep(w_ff, b_ff)


def kernel(x, mem, w_in, b_in, ssm_lambda_re, ssm_lambda_im, ssm_log_dt, ssm_b_re, ssm_b_im, ssm_c_re, ssm_c_im, ssm_d, ca_rel_bias, w_fox_o, w_ssm_glu, w_ca_o, w_o, xa_wq, xa_wkv, xa_wo, ffn_w_up, ffn_conv_w, ffn_conv_b, ffn_w_down, ln_g, ln_b):
    B, S, D = x.shape
    depth = w_in.shape[0]
    alpha = float((2 * depth) ** 0.25)
    mem_b = mem.astype(BF16)
    xf = x.astype(F32)
    xb = x.astype(BF16)
    zero_bias = jnp.zeros((1, 2 * D), F32)
    (w_qkv, b_qkv), (w_gate, b_gate), (w_su, b_su), (w_ff, b_ff) = _split_in_proj(w_in, b_in, D)
    w_fox_o, w_ssm_glu, w_ca_o, w_o, xa_wq, xa_wkv, xa_wo, ffn_w_up, ffn_w_down = (
        w.astype(BF16) for w in (w_fox_o, w_ssm_glu, w_ca_o, w_o, xa_wq, xa_wkv, xa_wo,
                                 ffn_w_up, ffn_w_down))
    for l in range(depth):
        z = _matmul(xb, w_qkv, l, b_qkv[l], BF16, tm=1024, name="in_proj")
        qfeat, kfeat = _forget_decay_features(xb, w_ff, l, b_ff[l])
        ya = _fox_attention(z, qfeat, kfeat, 0)
        yc = _chunk_attention(z, _ca_bias_vector(ca_rel_bias[l], ATT_BLOCK), 3 * FOX_WIDTH // LANES)
        bbr, bbi, cc, ar, ai = _s5_params(ssm_lambda_re[l], ssm_lambda_im[l], ssm_log_dt[l],
                                          ssm_b_re[l], ssm_b_im[l], ssm_c_re[l], ssm_c_im[l], B)
        yb = _s5_branch(xb, l, w_su, b_su[l], bbr, bbi, cc, ar, ai,
                        ssm_d[l].astype(F32)[None, :], w_ssm_glu)
        ln = lambda k: (ln_g[l, k].astype(F32)[None, :], ln_b[l, k].astype(F32)[None, :])
        xf, xb = _merge(xf, xb, ya, yb, yc, l, w_gate, b_gate[l], w_fox_o, w_ca_o, w_o,
                        *ln(0), alpha, tm=1024)
        kv = _matmul(mem_b, xa_wkv, l, zero_bias, BF16, tm=256, name="mem_kv")
        xf, xb = _cross_attention(xf, xb, kv, l, xa_wq, xa_wo, *ln(1), alpha, tm=1024)
        xf, xb = _conv_ffn(xf, xb, l, ffn_w_up, ffn_conv_w[l].astype(F32),
                           ffn_conv_b[l].astype(F32)[None, :], ffn_w_down, *ln(2), alpha, tm=512)
    return xf.astype(x.dtype)
```

```python
import functools
import math

import jax
import jax.numpy as jnp
import numpy as np
from jax import lax
from jax.experimental import pallas as pl
from jax.experimental.pallas import tpu as pltpu

F32 = jnp.float32
BF16 = jnp.bfloat16

CHUNK = 64
Q_BLOCK = 128
FOX_HEADS = 8
FOX_HEAD_DIM = 64
FOX_WIDTH = FOX_HEADS * FOX_HEAD_DIM
SSM_GROUP = 16
SSM_WIDTH = 512
SSM_GROUPS = SSM_WIDTH // SSM_GROUP
SSM_STATE = 64
CA_HEADS = 8
CA_HEAD_DIM = 64
CA_WIDTH = CA_HEADS * CA_HEAD_DIM
CA_LEFT_CHUNKS = 8
REL_MIN = -(CHUNK - 1)
REL_MAX = 4 * CHUNK
XA_HEADS = 4
CONV_WIDTH = 3
LN_EPS = 1e-5
NEG_INF = -1e30

LOG2E = math.log2(math.e)
DECAY_LANES = 6
DECAY_STRIDE = 8
LANES = 128
HEAD_PAIR = LANES // FOX_HEAD_DIM
ATT_BLOCK = 256
FOX_BLOCK = 512
CA_WINDOW_BLOCKS = 3
SSM_SLAB_GROUPS = LANES // SSM_GROUP
SSM_SLABS = SSM_GROUPS // SSM_SLAB_GROUPS
SSM_SLAB_STATE = SSM_SLAB_GROUPS * SSM_STATE
SSM_TIME_BLOCK = 128
MATMUL_SUB_COLS = 512
MERGE_SUB_COLS = 256
FFN_SUB_COLS = 256
FFN_DOWN_GROUP = 4
FFN_HALO_ROWS = 8
VMEM_LIMIT = 56 * 1024 * 1024

_ARB = "arbitrary"


def _params(n_axes):
    return pltpu.CompilerParams(dimension_semantics=(_ARB,) * n_axes,
                                vmem_limit_bytes=VMEM_LIMIT)


def _whole(a):
    return pl.BlockSpec(a.shape, lambda *_: (0,) * a.ndim)


def _layer(a, l, **kwargs):
    return pl.BlockSpec((None,) + a.shape[1:], lambda *_: (l,) + (0,) * (a.ndim - 1), **kwargs)


def _layer_norm(v, g, b):
    mu = jnp.mean(v, axis=-1, keepdims=True)
    d = v - mu
    var = jnp.mean(d * d, axis=-1, keepdims=True)
    return d * lax.rsqrt(var + LN_EPS) * g + b


def _dot(a, b):
    return jnp.dot(a, b, preferred_element_type=F32)


def _dot_nt(a, b):
    return lax.dot_general(a, b, (((1,), (1,)), ((), ())), preferred_element_type=F32)


def _mm_kernel(x_ref, w_ref, b_ref, o_ref):
    x = x_ref[...]
    n = w_ref.shape[1]
    sub = min(MATMUL_SUB_COLS, n)
    for c in range(n // sub):
        cols = slice(c * sub, (c + 1) * sub)
        o_ref[:, cols] = (_dot(x, w_ref[:, cols]) + b_ref[:, cols]).astype(o_ref.dtype)


def _matmul(x, w, l, bias, out_dtype, *, tm, name="matmul"):
    B, S, K = x.shape
    N = w.shape[2]
    tm = min(tm, S)
    return pl.pallas_call(
        _mm_kernel,
        out_shape=jax.ShapeDtypeStruct((B, S, N), out_dtype),
        grid=(B, S // tm),
        in_specs=[pl.BlockSpec((None, tm, K), lambda b, i: (b, i, 0)), _layer(w, l), _whole(bias)],
        out_specs=pl.BlockSpec((None, tm, N), lambda b, i: (b, i, 0)),
        compiler_params=_params(2),
        name=name,
    )(x, w, bias)


def _split3(c):
    hi = c.astype(BF16).astype(F32)
    r = c - hi
    mid = r.astype(BF16).astype(F32)
    lo = (r - mid).astype(BF16).astype(F32)
    return hi, mid, lo


def _decay_scatter(query_side):
    r = lax.broadcasted_iota(jnp.int32, (3 * LANES, LANES), 0)
    c = lax.broadcasted_iota(jnp.int32, (3 * LANES, LANES), 1)
    piece, h = r // LANES, r % LANES
    dst = DECAY_STRIDE * h + piece + (0 if query_side else 3)
    mat = jnp.where((h < FOX_HEADS) & (c == dst), 1.0 if query_side else -1.0, 0.0).astype(BF16)
    lane = lax.broadcasted_iota(jnp.int32, (1, LANES), 1)
    k = lane % DECAY_STRIDE
    ones_lo = 3 if query_side else 0
    ones = jnp.where((lane < DECAY_STRIDE * FOX_HEADS) & (k >= ones_lo) & (k < ones_lo + 3), 1.0, 0.0)
    return mat, ones


def _cum_kernel(x_ref, w_ref, b_ref, qf_ref, kf_ref, *, blk):
    S = x_ref.shape[0]
    row = lax.broadcasted_iota(jnp.int32, (blk, blk), 0)
    col = lax.broadcasted_iota(jnp.int32, (blk, blk), 1)
    tri = jnp.where(row >= col, 1.0, 0.0).astype(BF16)
    q_mat, q_ones = _decay_scatter(True)
    k_mat, k_ones = _decay_scatter(False)
    qk_mat = jnp.concatenate([q_mat, k_mat], axis=1)
    pieces = lambda v: jnp.concatenate([p.astype(BF16) for p in _split3(v)], axis=1)
    local = []
    for i in range(S // blk):
        x = _dot(x_ref[i * blk:(i + 1) * blk, :], w_ref[...]) + b_ref[...]
        ls = jnp.minimum(x, 0.0) - jnp.log(1.0 + jnp.exp(-jnp.abs(x)))
        t = _dot(tri, pieces(ls))
        local.append(t[:, :LANES] + t[:, LANES:2 * LANES] + t[:, 2 * LANES:])
    carry = jnp.zeros((1, LANES), F32)
    for i in range(S // blk):
        cs = local[i] + carry
        carry = cs[blk - 1:blk, :]
        feats = _dot(pieces(cs * LOG2E), qk_mat)
        qf_ref[i * blk:(i + 1) * blk, :] = feats[:, :LANES] + q_ones
        kf_ref[i * blk:(i + 1) * blk, :] = feats[:, LANES:] + k_ones


def _forget_decay_features(xb, w_ff, l, b_ff):
    B, S, D = xb.shape
    L = w_ff.shape[2]
    out = jax.ShapeDtypeStruct((B, S, L), F32)
    spec = pl.BlockSpec((None, S, L), lambda b: (b, 0, 0))
    return pl.pallas_call(
        functools.partial(_cum_kernel, blk=min(ATT_BLOCK, S)),
        out_shape=(out, out),
        grid=(B,),
        in_specs=[pl.BlockSpec((None, S, D), lambda b: (b, 0, 0)), _layer(w_ff, l), _whole(b_ff)],
        out_specs=(spec, spec),
        compiler_params=_params(1),
        name="forget_cumsum",
    )(xb, w_ff, b_ff)


def _head_lane_mask(j, shape):
    lane = lax.broadcasted_iota(jnp.int32, shape, len(shape) - 1)
    lo = j * FOX_HEAD_DIM
    return (lane >= lo) & (lane < lo + FOX_HEAD_DIM)


def _with_decay_lanes(x, feat, j, p):
    a0 = (1 - j) * FOX_HEAD_DIM
    shift = a0 - DECAY_STRIDE * j + (LANES if j else 0) - DECAY_STRIDE * HEAD_PAIR * p
    lane = lax.broadcasted_iota(jnp.int32, x.shape, 1)
    window = (lane >= a0) & (lane < a0 + DECAY_LANES)
    aug = jnp.where(window, pltpu.roll(feat, shift, 1), 0.0)
    return jnp.where(_head_lane_mask(j, x.shape), x, aug).astype(BF16)


def _fox_kernel(q_ref, k_ref, v_ref, qf_ref, kf_ref, o_ref, kaug_ref, vaug_ref, s_ref, mb_ref, *, blk):
    p = pl.program_id(1)
    S = k_ref.shape[0]
    n_blk = S // blk
    half = blk // 2
    tri = lambda qi, c: qi * (qi + 1) // 2 + c

    for r in range(n_blk):
        rows = slice(r * blk, (r + 1) * blk)
        kf = k_ref[rows, :].astype(F32)
        vf = v_ref[rows, :].astype(F32)
        feat = kf_ref[rows, :]
        for j in range(HEAD_PAIR):
            kaug_ref[j, rows, :] = _with_decay_lanes(kf, feat, j, p)
            vaug_ref[j, rows, :] = jnp.where(_head_lane_mask(j, vf.shape), vf, 1.0).astype(BF16)

    for qi in range(n_blk):
        rows = slice(qi * blk, (qi + 1) * blk)
        qf = q_ref[rows, :].astype(F32) * (FOX_HEAD_DIM ** -0.5 * LOG2E)
        feat = qf_ref[rows, :]
        for j in range(HEAD_PAIR):
            qaug = _with_decay_lanes(qf, feat, j, p)
            ml = [None, None]

            def fold(ml, s, h):
                for g in range(s.shape[1] // LANES):
                    sg = s[:, g * LANES:(g + 1) * LANES]
                    ml[h] = sg if ml[h] is None else jnp.maximum(ml[h], sg)

            for c in range(qi):
                s = _dot_nt(qaug, kaug_ref[j, c * blk:(c + 1) * blk, :])
                s_ref[tri(qi, c), j] = s
                fold(ml, s[:half], 0)
                fold(ml, s[half:], 1)
            k_diag = kaug_ref[j, qi * blk:(qi + 1) * blk, :]
            row = lax.broadcasted_iota(jnp.int32, (half, half), 0)
            colk = lax.broadcasted_iota(jnp.int32, (half, half), 1)
            s_top = jnp.where(row >= colk, _dot_nt(qaug[:half], k_diag[:half]), NEG_INF)
            s_bot = _dot_nt(qaug[half:], k_diag)
            s_bot = jnp.concatenate([s_bot[:, :half], jnp.where(row >= colk, s_bot[:, half:], NEG_INF)],
                                    axis=1)
            s_ref[tri(qi, qi), j, :half, :half] = s_top
            s_ref[tri(qi, qi), j, half:, :] = s_bot
            fold(ml, s_top, 0)
            fold(ml, s_bot, 1)
            mb_ref[qi, j] = jnp.concatenate(
                [jnp.broadcast_to(jnp.max(m, axis=1, keepdims=True), (half, LANES)) for m in ml], axis=0)

    for qi in range(n_blk):
        outs = []
        for j in range(HEAD_PAIR):
            mb = jnp.concatenate([mb_ref[qi, j]] * (blk // LANES), axis=1)
            acc = None
            for c in range(qi):
                pr = jnp.exp2((s_ref[tri(qi, c), j] - mb).astype(BF16))
                d = _dot(pr, vaug_ref[j, c * blk:(c + 1) * blk, :])
                acc = d if acc is None else acc + d
            diag = tri(qi, qi)
            pr_top = jnp.exp2((s_ref[diag, j, :half, :half] - mb[:half, :half]).astype(BF16))
            pr_bot = jnp.exp2((s_ref[diag, j, half:, :] - mb[half:, :]).astype(BF16))
            d = jnp.concatenate([_dot(pr_top, vaug_ref[j, qi * blk:qi * blk + half, :]),
                                 _dot(pr_bot, vaug_ref[j, qi * blk:(qi + 1) * blk, :])], axis=0)
            acc = d if acc is None else acc + d
            outs.append(acc / pltpu.roll(acc, FOX_HEAD_DIM, 1))
        o_ref[qi * blk:(qi + 1) * blk, :] = jnp.where(
            _head_lane_mask(0, outs[0].shape), outs[0], outs[1]).astype(o_ref.dtype)


def _fox_attention(z, qfeat, kfeat, base):
    B, S, _ = z.shape
    blk = min(FOX_BLOCK, S)
    n_q = S // blk
    n_pair = FOX_WIDTH // LANES
    return pl.pallas_call(
        functools.partial(_fox_kernel, blk=blk),
        out_shape=jax.ShapeDtypeStruct((B, S, FOX_WIDTH), BF16),
        grid=(B, n_pair),
        in_specs=[pl.BlockSpec((None, S, LANES), lambda b, p: (b, 0, base + p)),
                  pl.BlockSpec((None, S, LANES), lambda b, p: (b, 0, base + n_pair + p)),
                  pl.BlockSpec((None, S, LANES), lambda b, p: (b, 0, base + 2 * n_pair + p)),
                  pl.BlockSpec((None, S, LANES), lambda b, p: (b, 0, 0)),
                  pl.BlockSpec((None, S, LANES), lambda b, p: (b, 0, 0))],
        out_specs=pl.BlockSpec((None, S, LANES), lambda b, p: (b, 0, p)),
        scratch_shapes=[pltpu.VMEM((HEAD_PAIR, S, LANES), BF16),
                        pltpu.VMEM((HEAD_PAIR, S, LANES), BF16),
                        pltpu.VMEM((n_q * (n_q + 1) // 2, HEAD_PAIR, blk, blk), F32),
                        pltpu.VMEM((n_q, HEAD_PAIR, blk, LANES), F32)],
        compiler_params=_params(2),
        name="fox_attention",
    )(z, z, z, qfeat, kfeat)


def _ca_kernel(q_ref, k_ref, v_ref, vec_ref, o_ref, u_ref, vaug_ref, s_ref, mb_ref, *, blk):
    S = q_ref.shape[0]
    left = CA_WINDOW_BLOCKS - 1
    n_blk = S // blk
    kblocks = [[(j - left + i, i) for i in range(CA_WINDOW_BLOCKS) if j - left + i >= 0]
               for j in range(n_blk)]

    @pl.when(pl.program_id(1) == 0)
    def _build_table():
        row_chunk = lax.broadcasted_iota(jnp.int32, (blk, blk), 0) // CHUNK
        col_chunk = lax.broadcasted_iota(jnp.int32, (blk, blk), 1) // CHUNK
        for hh in range(HEAD_PAIR):
            vec = jnp.broadcast_to(vec_ref[hh:hh + 1, :], (blk, vec_ref.shape[1]))
            toep = pltpu.roll(vec, 0, 1, stride=1, stride_axis=0)
            for i in range(CA_WINDOW_BLOCKS):
                dchunk = row_chunk - col_chunk + (left - i) * (blk // CHUNK)
                valid = (dchunk >= 0) & (dchunk <= CA_LEFT_CHUNKS)
                u_ref[hh, i] = jnp.where(valid, toep[:, i * blk:(i + 1) * blk], NEG_INF)

    for r in range(n_blk):
        rows = slice(r * blk, (r + 1) * blk)
        vf = v_ref[rows, :].astype(F32)
        for hh in range(HEAD_PAIR):
            vaug_ref[hh, rows, :] = jnp.where(_head_lane_mask(hh, vf.shape), vf, 1.0).astype(BF16)

    for j in range(n_blk):
        qf = q_ref[j * blk:(j + 1) * blk, :].astype(F32) * (CA_HEAD_DIM ** -0.5 * LOG2E)
        for hh in range(HEAD_PAIR):
            qm = jnp.where(_head_lane_mask(hh, qf.shape), qf, 0.0).astype(BF16)
            ml = None
            for kb, i in kblocks[j]:
                s = _dot_nt(qm, k_ref[kb * blk:(kb + 1) * blk, :]) + u_ref[hh, i]
                s_ref[j, hh, i] = s
                for g in range(blk // LANES):
                    sg = s[:, g * LANES:(g + 1) * LANES]
                    ml = sg if ml is None else jnp.maximum(ml, sg)
            mb_ref[j, hh] = jnp.broadcast_to(jnp.max(ml, axis=1, keepdims=True), (blk, LANES))

    for j in range(n_blk):
        outs = []
        for hh in range(HEAD_PAIR):
            mb = jnp.concatenate([mb_ref[j, hh]] * (blk // LANES), axis=1)
            acc = None
            for kb, i in kblocks[j]:
                pr = jnp.exp2((s_ref[j, hh, i] - mb).astype(BF16))
                d = _dot(pr, vaug_ref[hh, kb * blk:(kb + 1) * blk, :])
                acc = d if acc is None else acc + d
            outs.append(acc / pltpu.roll(acc, CA_HEAD_DIM, 1))
        o_ref[j * blk:(j + 1) * blk, :] = jnp.where(
            _head_lane_mask(0, outs[0].shape), outs[0], outs[1]).astype(o_ref.dtype)


def _ca_bias_vector(rel_bias, blk):
    n_m = CA_WINDOW_BLOCKS * blk
    shift = (CA_WINDOW_BLOCKS - 1) * blk
    period = n_m + blk
    x = np.arange(period)
    x = np.where(x >= n_m, x - period, x)
    vec = rel_bias.astype(F32)[:, np.clip(shift - x, REL_MIN, REL_MAX) - REL_MIN] * LOG2E
    return vec.reshape(rel_bias.shape[0] // HEAD_PAIR, HEAD_PAIR, period)


def _chunk_attention(z, bias_vec, base):
    B, S, _ = z.shape
    blk = ATT_BLOCK
    n_pair = CA_WIDTH // LANES
    period = bias_vec.shape[2]
    return pl.pallas_call(
        functools.partial(_ca_kernel, blk=blk),
        out_shape=jax.ShapeDtypeStruct((B, S, CA_WIDTH), BF16),
        grid=(n_pair, B),
        in_specs=[pl.BlockSpec((None, S, LANES), lambda p, b: (b, 0, base + p)),
                  pl.BlockSpec((None, S, LANES), lambda p, b: (b, 0, base + n_pair + p)),
                  pl.BlockSpec((None, S, LANES), lambda p, b: (b, 0, base + 2 * n_pair + p)),
                  pl.BlockSpec((None, HEAD_PAIR, period), lambda p, b: (p, 0, 0))],
        out_specs=pl.BlockSpec((None, S, LANES), lambda p, b: (b, 0, p)),
        scratch_shapes=[pltpu.VMEM((HEAD_PAIR, CA_WINDOW_BLOCKS, blk, blk), F32),
                        pltpu.VMEM((HEAD_PAIR, S, LANES), BF16),
                        pltpu.VMEM((S // blk, HEAD_PAIR, CA_WINDOW_BLOCKS, blk, blk), F32),
                        pltpu.VMEM((S // blk, HEAD_PAIR, blk, LANES), F32)],
        compiler_params=_params(2),
        name="chunk_attention",
    )(z, z, z, bias_vec)


def _s5_kernel(x_ref, wsu_ref, bsu_ref, bbr_ref, bbi_ref, cc_ref, ar_ref, ai_ref, d_ref, wg_ref, o_ref,
               ut_ref, xr_ref, xi_ref, yt_ref, hr_ref, hi_ref, *, steps, batch):
    @pl.when(pl.program_id(0) == 0)
    def _init():
        hr_ref[...] = jnp.zeros_like(hr_ref)
        hi_ref[...] = jnp.zeros_like(hi_ref)

    for b in range(batch):
        su = _dot(x_ref[b], wsu_ref[...]) + bsu_ref[...]
        for i in range(SSM_SLABS):
            ut_ref[i, pl.ds(b, steps, stride=batch), :] = su[:, i * LANES:(i + 1) * LANES]

    slab = lambda i: slice(i * SSM_SLAB_STATE, (i + 1) * SSM_SLAB_STATE)
    for i in range(SSM_SLABS):
        ui = ut_ref[i].astype(BF16)
        xr_ref[:, slab(i)] = _dot(ui, bbr_ref[i])
        xi_ref[:, slab(i)] = _dot(ui, bbi_ref[i])
    for i in range(SSM_SLABS):
        cols = slab(i)
        ar = ar_ref[:, cols]
        ai = ai_ref[:, cols]
        hr = hr_ref[:, cols]
        hi = hi_ref[:, cols]
        for t in range(steps):
            rows = slice(t * batch, (t + 1) * batch)
            hr, hi = (ar * hr - ai * hi + xr_ref[rows, cols], ar * hi + ai * hr + xi_ref[rows, cols])
            xr_ref[rows, cols] = hr
            xi_ref[rows, cols] = hi
        hr_ref[:, cols] = hr
        hi_ref[:, cols] = hi
    ys = []
    for i in range(SSM_SLABS):
        cols = slab(i)
        h = jnp.concatenate([xr_ref[:, cols].astype(BF16), xi_ref[:, cols].astype(BF16)], axis=1)
        ys.append(_dot(h, cc_ref[i]) + d_ref[:, i * LANES:(i + 1) * LANES] * ut_ref[i])
    g = jax.nn.gelu(jnp.concatenate(ys, axis=1)).astype(BF16)
    half = wg_ref.shape[1] // 2
    for c in range(half // MERGE_SUB_COLS):
        cols = slice(c * MERGE_SUB_COLS, (c + 1) * MERGE_SUB_COLS)
        gcols = slice(half + c * MERGE_SUB_COLS, half + (c + 1) * MERGE_SUB_COLS)
        out = _dot(g, wg_ref[:, cols]) * jax.nn.sigmoid(_dot(g, wg_ref[:, gcols]))
        for k in range(MERGE_SUB_COLS // LANES):
            yt_ref[c * (MERGE_SUB_COLS // LANES) + k] = out[:, k * LANES:(k + 1) * LANES]
    for b in range(batch):
        for k in range(half // LANES):
            o_ref[b, :, k * LANES:(k + 1) * LANES] = (
                yt_ref[k, pl.ds(b, steps, stride=batch), :].astype(o_ref.dtype))


def _s5_branch(xb, l, w_su, b_su, bbr, bbi, cc, ar, ai, d, w_glu):
    B, S, D = xb.shape
    steps = min(SSM_TIME_BLOCK, S)
    tr = steps * B
    n_state = SSM_GROUPS * SSM_STATE
    d_out = w_glu.shape[2] // 2
    return pl.pallas_call(
        functools.partial(_s5_kernel, steps=steps, batch=B),
        out_shape=jax.ShapeDtypeStruct((B, S, d_out), BF16),
        grid=(S // steps,),
        in_specs=[pl.BlockSpec((B, steps, D), lambda t: (0, t, 0)),
                  _layer(w_su, l), _whole(b_su), _whole(bbr), _whole(bbi), _whole(cc), _whole(ar),
                  _whole(ai), _whole(d), _layer(w_glu, l)],
        out_specs=pl.BlockSpec((B, steps, d_out), lambda t: (0, t, 0)),
        scratch_shapes=[pltpu.VMEM((SSM_SLABS, tr, LANES), F32),
                        pltpu.VMEM((tr, n_state), F32), pltpu.VMEM((tr, n_state), F32),
                        pltpu.VMEM((d_out // LANES, tr, LANES), F32),
                        pltpu.VMEM((B, n_state), F32), pltpu.VMEM((B, n_state), F32)],
        compiler_params=_params(1),
        name="s5_branch",
    )(xb, w_su, b_su, bbr, bbi, cc, ar, ai, d, w_glu)


def _s5_params(lam_re, lam_im, log_dt, b_re, b_im, c_re, c_im, batch):
    lr = jnp.minimum(lam_re.astype(F32), -1e-4)
    li = lam_im.astype(F32)
    dt = jnp.exp(log_dt.astype(F32))[:, None]
    mag = jnp.exp(lr * dt)
    ar = mag * jnp.cos(li * dt)
    ai = mag * jnp.sin(li * dt)
    den = lr * lr + li * li
    gr = ((ar - 1.0) * lr + ai * li) / den
    gi = (ai * lr - (ar - 1.0) * li) / den
    br, bi = b_re.astype(F32), b_im.astype(F32)
    bbr = gr[..., None] * br - gi[..., None] * bi
    bbi = gr[..., None] * bi + gi[..., None] * br
    eye = jnp.eye(SSM_SLAB_GROUPS, dtype=F32)

    def in_map(bb):
        bb = bb.reshape(SSM_SLABS, SSM_SLAB_GROUPS, SSM_STATE, SSM_GROUP)
        return jnp.einsum('sgpc,gh->sgchp', bb, eye).reshape(SSM_SLABS, LANES, SSM_SLAB_STATE).astype(BF16)

    def out_map(c):
        c = c.astype(F32).reshape(SSM_SLABS, SSM_SLAB_GROUPS, SSM_GROUP, SSM_STATE)
        return jnp.einsum('sgcp,gh->sgphc', c, eye).reshape(SSM_SLABS, SSM_SLAB_STATE, LANES)

    cc = jnp.concatenate([out_map(c_re), -out_map(c_im)], axis=1).astype(BF16)
    n_state = SSM_GROUPS * SSM_STATE
    ar_b = jnp.broadcast_to(ar.reshape(1, n_state), (batch, n_state))
    ai_b = jnp.broadcast_to(ai.reshape(1, n_state), (batch, n_state))
    return in_map(bbr), in_map(bbi), cc, ar_b, ai_b


def _merge_kernel(x_ref, xb_ref, ya_ref, yb_ref, yc_ref, wg_ref, bg_ref, wa_ref, wc_ref, wo_ref,
                  lg_ref, lb_ref, of_ref, ob_ref, m_ref, *, alpha):
    d = x_ref.shape[1]
    xb = xb_ref[...]
    ya = ya_ref[...]
    yc = yc_ref[...]
    for c in range(d // MERGE_SUB_COLS):
        cols = slice(c * MERGE_SUB_COLS, (c + 1) * MERGE_SUB_COLS)

        def gate(k, cols=cols):
            gcols = slice(k * d + cols.start, k * d + cols.stop)
            return jax.nn.sigmoid(_dot(xb, wg_ref[:, gcols]) + bg_ref[:, gcols])

        merged = (gate(0) * _dot(ya, wa_ref[:, cols])
                  + gate(1) * yb_ref[:, cols].astype(F32)
                  + gate(2) * _dot(yc, wc_ref[:, cols]))
        m_ref[:, cols] = merged.astype(BF16)
    h = _dot(m_ref[...], wo_ref[...])
    y = _layer_norm(alpha * x_ref[...] + h, lg_ref[...], lb_ref[...])
    of_ref[...] = y
    ob_ref[...] = y.astype(BF16)


def _merge(x, xb, ya, yb, yc, l, w_gate, b_gate, w_fox_o, w_ca_o, w_o, lg, lb, alpha, tm):
    B, S, D = x.shape
    tm = min(tm, S)
    row = lambda w: pl.BlockSpec((None, tm, w), lambda b, i: (b, i, 0))
    return pl.pallas_call(
        functools.partial(_merge_kernel, alpha=alpha),
        out_shape=(jax.ShapeDtypeStruct((B, S, D), F32), jax.ShapeDtypeStruct((B, S, D), BF16)),
        grid=(B, S // tm),
        in_specs=[row(D), row(D), row(ya.shape[2]), row(D), row(yc.shape[2]),
                  _layer(w_gate, l), _whole(b_gate), _layer(w_fox_o, l), _layer(w_ca_o, l),
                  _layer(w_o, l), _whole(lg), _whole(lb)],
        out_specs=(row(D), row(D)),
        scratch_shapes=[pltpu.VMEM((tm, D), BF16)],
        compiler_params=_params(2),
        name="merge_proj_ln",
    )(x, xb, ya, yb, yc, w_gate, b_gate, w_fox_o, w_ca_o, w_o, lg, lb)


def _xattn_kernel(x_ref, xb_ref, kv_ref, wq_ref, wo_ref, lg_ref, lb_ref, of_ref, ob_ref, *, alpha):
    d = x_ref.shape[1]
    dh = d // XA_HEADS
    xb = xb_ref[...]
    heads = range(XA_HEADS)
    head = lambda h: slice(h * dh, (h + 1) * dh)
    q = [(_dot(xb, wq_ref[:, head(h)]) * (dh ** -0.5 * LOG2E)).astype(BF16) for h in heads]
    s = [_dot_nt(q[h], kv_ref[:, head(h)]) for h in heads]
    m = [jnp.max(s[h], axis=-1, keepdims=True) for h in heads]
    pr = [jnp.exp2(s[h] - m[h]) for h in heads]
    l = [jnp.sum(pr[h], axis=-1, keepdims=True) for h in heads]
    outs = [(_dot(pr[h].astype(BF16), kv_ref[:, d + h * dh:d + (h + 1) * dh]) / l[h]).astype(BF16)
            for h in heads]
    hout = _dot(jnp.concatenate(outs, axis=1), wo_ref[...])
    y = _layer_norm(alpha * x_ref[...] + hout, lg_ref[...], lb_ref[...])
    of_ref[...] = y
    ob_ref[...] = y.astype(BF16)


def _cross_attention(x, xb, kv, l, wq, wo, lg, lb, alpha, tm):
    B, S, D = x.shape
    M = kv.shape[1]
    tm = min(tm, S)
    row = pl.BlockSpec((None, tm, D), lambda b, i: (b, i, 0))
    return pl.pallas_call(
        functools.partial(_xattn_kernel, alpha=alpha),
        out_shape=(jax.ShapeDtypeStruct((B, S, D), F32), jax.ShapeDtypeStruct((B, S, D), BF16)),
        grid=(B, S // tm),
        in_specs=[row, row, pl.BlockSpec((None, M, 2 * D), lambda b, i: (b, 0, 0)),
                  _layer(wq, l), _layer(wo, l), _whole(lg), _whole(lb)],
        out_specs=(row, row),
        compiler_params=_params(2),
        name="cross_attention_ln",
    )(x, xb, kv, wq, wo, lg, lb)


def _ffn_kernel(x_ref, xb_ref, wup_ref, cw_ref, cb_ref, wdn_ref, lg_ref, lb_ref, of_ref, ob_ref,
                abuf_ref, gbuf_ref, act_ref, *, alpha, d_ff):
    tm = x_ref.shape[0]
    pad = FFN_HALO_ROWS

    @pl.when(pl.program_id(1) == 0)
    def _seq_start():
        abuf_ref[0:pad, :] = jnp.zeros((pad, d_ff), F32)

    xb = xb_ref[...]
    h = jnp.zeros(x_ref.shape, F32)
    n_sub = d_ff // FFN_SUB_COLS
    for j in range(n_sub):
        cols = slice(j * FFN_SUB_COLS, (j + 1) * FFN_SUB_COLS)
        gcols = slice(d_ff + j * FFN_SUB_COLS, d_ff + (j + 1) * FFN_SUB_COLS)
        abuf_ref[pad:pad + tm, cols] = _dot(xb, wup_ref[:, cols])
        gbuf_ref[:, cols] = _dot(xb, wup_ref[:, gcols])
    for j in range(n_sub):
        cols = slice(j * FFN_SUB_COLS, (j + 1) * FFN_SUB_COLS)
        a = abuf_ref[pad:pad + tm, cols]
        a1 = abuf_ref[pad - 1:pad - 1 + tm, cols]
        a2 = abuf_ref[pad - 2:pad - 2 + tm, cols]
        conv = (cw_ref[0:1, cols] * a2 + cw_ref[1:2, cols] * a1 + cw_ref[2:3, cols] * a
                + cb_ref[:, cols])
        act_ref[:, cols] = (jax.nn.gelu(conv) * gbuf_ref[:, cols]).astype(BF16)
        if (j + 1) % FFN_DOWN_GROUP == 0 or j == n_sub - 1:
            lo = (j // FFN_DOWN_GROUP) * FFN_DOWN_GROUP * FFN_SUB_COLS
            rows = slice(lo, (j + 1) * FFN_SUB_COLS)
            h = h + _dot(act_ref[:, rows], wdn_ref[rows, :])
    abuf_ref[0:pad, :] = abuf_ref[tm:tm + pad, :]
    y = _layer_norm(alpha * x_ref[...] + h, lg_ref[...], lb_ref[...])
    of_ref[...] = y
    ob_ref[...] = y.astype(BF16)


def _conv_ffn(x, xb, l, w_up, conv_w, conv_b, w_down, lg, lb, alpha, tm):
    B, S, D = x.shape
    d_ff = w_down.shape[1]
    tm = min(tm, S)
    row = pl.BlockSpec((None, tm, D), lambda b, i: (b, i, 0))
    once = pl.Buffered(1)
    return pl.pallas_call(
        functools.partial(_ffn_kernel, alpha=alpha, d_ff=d_ff),
        out_shape=(jax.ShapeDtypeStruct((B, S, D), F32), jax.ShapeDtypeStruct((B, S, D), BF16)),
        grid=(B, S // tm),
        in_specs=[row, row, _layer(w_up, l, pipeline_mode=once), _whole(conv_w), _whole(conv_b),
                  _layer(w_down, l, pipeline_mode=once), _whole(lg), _whole(lb)],
        out_specs=(row, row),
        scratch_shapes=[pltpu.VMEM((tm + FFN_HALO_ROWS, d_ff), F32), pltpu.VMEM((tm, d_ff), F32),
                        pltpu.VMEM((tm, d_ff), BF16)],
        compiler_params=_params(2),
        name="conv_ffn_ln",
    )(x, xb, w_up, conv_w, conv_b, w_down, lg, lb)


def _split_in_proj(w_in, b_in, d_model):
    o = np.cumsum([0, FOX_WIDTH, FOX_WIDTH, FOX_WIDTH, FOX_HEADS, SSM_WIDTH,
                   CA_WIDTH, CA_WIDTH, CA_WIDTH, 3 * d_model])
    fqkv, ff, su, cqkv, gates = (slice(o[0], o[3]), slice(o[3], o[4]), slice(o[4], o[5]),
                                 slice(o[5], o[8]), slice(o[8], o[9]))
    cols = lambda s: (w_in[..., s].astype(BF16), b_in[..., s].astype(F32)[:, None, :])
    cat = lambda a, b: tuple(jnp.concatenate(p, axis=-1) for p in zip(a, b))
    pad = ((0, 0), (0, 0), (0, LANES - FOX_HEADS))
    w_ff, b_ff = cols(ff)
    return cat(cols(fqkv), cols(cqkv)), cols(gates), cols(su), (jnp.pad(w_ff, pad), jnp.pad(b_ff, pad))


def kernel(x, mem, w_in, b_in, ssm_lambda_re, ssm_lambda_im, ssm_log_dt, ssm_b_re, ssm_b_im, ssm_c_re, ssm_c_im, ssm_d, ca_rel_bias, w_fox_o, w_ssm_glu, w_ca_o, w_o, xa_wq, xa_wkv, xa_wo, ffn_w_up, ffn_conv_w, ffn_conv_b, ffn_w_down, ln_g, ln_b):
    B, S, D = x.shape
    depth = w_in.shape[0]
    alpha = float((2 * depth) ** 0.25)
    mem_b = mem.astype(BF16)
    xf = x.astype(F32)
    xb = x.astype(BF16)
    zero_bias = jnp.zeros((1, 2 * D), F32)
    (w_qkv, b_qkv), (w_gate, b_gate), (w_su, b_su), (w_ff, b_ff) = _split_in_proj(w_in, b_in, D)
    w_fox_o, w_ssm_glu, w_ca_o, w_o, xa_wq, xa_wkv, xa_wo, ffn_w_up, ffn_w_down = (
        w.astype(BF16) for w in (w_fox_o, w_ssm_glu, w_ca_o, w_o, xa_wq, xa_wkv, xa_wo,
                                 ffn_w_up, ffn_w_down))
    for l in range(depth):
        z = _matmul(xb, w_qkv, l, b_qkv[l], BF16, tm=1024, name="in_proj")
        qfeat, kfeat = _forget_decay_features(xb, w_ff, l, b_ff[l])
        ya = _fox_attention(z, qfeat, kfeat, 0)
        yc = _chunk_attention(z, _ca_bias_vector(ca_rel_bias[l], ATT_BLOCK), 3 * FOX_WIDTH // LANES)
        bbr, bbi, cc, ar, ai = _s5_params(ssm_lambda_re[l], ssm_lambda_im[l], ssm_log_dt[l],
                                          ssm_b_re[l], ssm_b_im[l], ssm_c_re[l], ssm_c_im[l], B)
        yb = _s5_branch(xb, l, w_su, b_su[l], bbr, bbi, cc, ar, ai,
                        ssm_d[l].astype(F32)[None, :], w_ssm_glu)
        ln = lambda k: (ln_g[l, k].astype(F32)[None, :], ln_b[l, k].astype(F32)[None, :])
        xf, xb = _merge(xf, xb, ya, yb, yc, l, w_gate, b_gate[l], w_fox_o, w_ca_o, w_o,
                        *ln(0), alpha, tm=1024)
        kv = _matmul(mem_b, xa_wkv, l, zero_bias, BF16, tm=256, name="mem_kv")
        xf, xb = _cross_attention(xf, xb, kv, l, xa_wq, xa_wo, *ln(1), alpha, tm=1024)
        xf, xb = _conv_ffn(xf, xb, l, ffn_w_up, ffn_conv_w[l].astype(F32),
                           ffn_conv_b[l].astype(F32)[None, :], ffn_w_down, *ln(2), alpha, tm=512)
    return xf.astype(x.dtype)
```

```python
import functools
import math

import jax
import jax.numpy as jnp
import numpy as np
from jax import lax
from jax.experimental import pallas as pl
from jax.experimental.pallas import tpu as pltpu

F32 = jnp.float32
BF16 = jnp.bfloat16

CHUNK = 64
FOX_HEADS = 8
FOX_HEAD_DIM = 64
FOX_WIDTH = FOX_HEADS * FOX_HEAD_DIM
SSM_GROUP = 16
SSM_WIDTH = 512
SSM_GROUPS = SSM_WIDTH // SSM_GROUP
SSM_STATE = 64
CA_HEADS = 8
CA_HEAD_DIM = 64
CA_WIDTH = CA_HEADS * CA_HEAD_DIM
CA_LEFT_CHUNKS = 8
REL_MIN = -(CHUNK - 1)
REL_MAX = 4 * CHUNK
XA_HEADS = 4
LN_EPS = 1e-5
NEG_INF = -1e30

LOG2E = math.log2(math.e)
DECAY_LANES = 6
DECAY_STRIDE = 8
LANES = 128
SUBLANES = 8
HEAD_PAIR = LANES // FOX_HEAD_DIM
ATT_BLOCK = 256
FOX_BLOCK = 512
CA_WINDOW_BLOCKS = 3
SSM_SLAB_GROUPS = LANES // SSM_GROUP
SSM_SLABS = SSM_GROUPS // SSM_SLAB_GROUPS
SSM_SLAB_STATE = SSM_SLAB_GROUPS * SSM_STATE
SSM_TIME_BLOCK = 128
MATMUL_SUB_COLS = 512
MERGE_SUB_COLS = 256
FFN_SUB_COLS = 256
FFN_DOWN_GROUP = 4
FFN_HALO_ROWS = 8
VMEM_LIMIT = 56 * 1024 * 1024

_ARB = "arbitrary"


def _params(n_axes):
    return pltpu.CompilerParams(dimension_semantics=(_ARB,) * n_axes,
                                vmem_limit_bytes=VMEM_LIMIT)


def _whole(a):
    return pl.BlockSpec(a.shape, lambda *_: (0,) * a.ndim)


def _layer(a, l, **kwargs):
    return pl.BlockSpec((None,) + a.shape[1:], lambda *_: (l,) + (0,) * (a.ndim - 1), **kwargs)


def _layer_norm(v, g, b):
    mu = jnp.mean(v, axis=-1, keepdims=True)
    d = v - mu
    var = jnp.mean(d * d, axis=-1, keepdims=True)
    return d * lax.rsqrt(var + LN_EPS) * g + b


def _dot(a, b):
    return jnp.dot(a, b, preferred_element_type=F32)


def _dot_nt(a, b):
    return lax.dot_general(a, b, (((1,), (1,)), ((), ())), preferred_element_type=F32)


def _mm_kernel(x_ref, w_ref, b_ref, o_ref):
    x = x_ref[...]
    n = w_ref.shape[1]
    sub = min(MATMUL_SUB_COLS, n)
    for c in range(n // sub):
        cols = slice(c * sub, (c + 1) * sub)
        o_ref[:, cols] = (_dot(x, w_ref[:, cols]) + b_ref[:, cols]).astype(o_ref.dtype)


def _matmul(x, w, l, bias, out_dtype, *, tm, name="matmul"):
    B, S, K = x.shape
    N = w.shape[2]
    tm = min(tm, S)
    return pl.pallas_call(
        _mm_kernel,
        out_shape=jax.ShapeDtypeStruct((B, S, N), out_dtype),
        grid=(B, S // tm),
        in_specs=[pl.BlockSpec((None, tm, K), lambda b, i: (b, i, 0)), _layer(w, l), _whole(bias)],
        out_specs=pl.BlockSpec((None, tm, N), lambda b, i: (b, i, 0)),
        compiler_params=_params(2),
        name=name,
    )(x, w, bias)


def _memory_kv(mem, w_kv, bias):
    B, M, K = mem.shape
    L, _, N = w_kv.shape
    return pl.pallas_call(
        _mm_kernel,
        out_shape=jax.ShapeDtypeStruct((L, B, M, N), BF16),
        grid=(L, B),
        in_specs=[pl.BlockSpec((None, M, K), lambda l, b: (b, 0, 0)),
                  pl.BlockSpec((None, K, N), lambda l, b: (l, 0, 0)), _whole(bias)],
        out_specs=pl.BlockSpec((None, None, M, N), lambda l, b: (l, b, 0, 0)),
        compiler_params=_params(2),
        name="mem_kv",
    )(mem, w_kv, bias)


def _split3(c):
    hi = c.astype(BF16).astype(F32)
    r = c - hi
    mid = r.astype(BF16).astype(F32)
    lo = (r - mid).astype(BF16).astype(F32)
    return hi, mid, lo


def _decay_scatter(query_side):
    r = lax.broadcasted_iota(jnp.int32, (3 * LANES, LANES), 0)
    c = lax.broadcasted_iota(jnp.int32, (3 * LANES, LANES), 1)
    piece, h = r // LANES, r % LANES
    dst = DECAY_STRIDE * h + piece + (0 if query_side else 3)
    mat = jnp.where((h < FOX_HEADS) & (c == dst), 1.0 if query_side else -1.0, 0.0).astype(BF16)
    lane = lax.broadcasted_iota(jnp.int32, (1, LANES), 1)
    k = lane % DECAY_STRIDE
    ones_lo = 3 if query_side else 0
    ones = jnp.where((lane < DECAY_STRIDE * FOX_HEADS) & (k >= ones_lo) & (k < ones_lo + 3), 1.0, 0.0)
    return mat, ones


def _cum_kernel(x_ref, w_ref, b_ref, qf_ref, kf_ref, *, blk):
    S = x_ref.shape[0]
    row = lax.broadcasted_iota(jnp.int32, (blk, blk), 0)
    col = lax.broadcasted_iota(jnp.int32, (blk, blk), 1)
    tri = jnp.where(row >= col, 1.0, 0.0).astype(BF16)
    q_mat, q_ones = _decay_scatter(True)
    k_mat, k_ones = _decay_scatter(False)
    qk_mat = jnp.concatenate([q_mat, k_mat], axis=1)
    pieces = lambda v: jnp.concatenate([p.astype(BF16) for p in _split3(v)], axis=1)
    local = []
    for i in range(S // blk):
        x = _dot(x_ref[i * blk:(i + 1) * blk, :], w_ref[...]) + b_ref[...]
        ls = jnp.minimum(x, 0.0) - jnp.log(1.0 + jnp.exp(-jnp.abs(x)))
        t = _dot(tri, pieces(ls))
        local.append(t[:, :LANES] + t[:, LANES:2 * LANES] + t[:, 2 * LANES:])
    carry = jnp.zeros((1, LANES), F32)
    for i in range(S // blk):
        cs = local[i] + carry
        carry = cs[blk - 1:blk, :]
        feats = _dot(pieces(cs * LOG2E), qk_mat)
        qf_ref[i * blk:(i + 1) * blk, :] = feats[:, :LANES] + q_ones
        kf_ref[i * blk:(i + 1) * blk, :] = feats[:, LANES:] + k_ones


def _forget_decay_features(xb, w_ff, l, b_ff):
    B, S, D = xb.shape
    L = w_ff.shape[2]
    out = jax.ShapeDtypeStruct((B, S, L), F32)
    spec = pl.BlockSpec((None, S, L), lambda b: (b, 0, 0))
    return pl.pallas_call(
        functools.partial(_cum_kernel, blk=min(ATT_BLOCK, S)),
        out_shape=(out, out),
        grid=(B,),
        in_specs=[pl.BlockSpec((None, S, D), lambda b: (b, 0, 0)), _layer(w_ff, l), _whole(b_ff)],
        out_specs=(spec, spec),
        compiler_params=_params(1),
        name="forget_cumsum",
    )(xb, w_ff, b_ff)


def _head_lane_mask(j, shape):
    lane = lax.broadcasted_iota(jnp.int32, shape, len(shape) - 1)
    lo = j * FOX_HEAD_DIM
    return (lane >= lo) & (lane < lo + FOX_HEAD_DIM)


def _with_decay_lanes(x, feat, j, p):
    a0 = (1 - j) * FOX_HEAD_DIM
    shift = a0 - DECAY_STRIDE * j + (LANES if j else 0) - DECAY_STRIDE * HEAD_PAIR * p
    lane = lax.broadcasted_iota(jnp.int32, x.shape, 1)
    window = (lane >= a0) & (lane < a0 + DECAY_LANES)
    aug = jnp.where(window, pltpu.roll(feat, shift, 1), 0.0)
    return jnp.where(_head_lane_mask(j, x.shape), x, aug).astype(BF16)


def _fox_kernel(q_ref, k_ref, v_ref, qf_ref, kf_ref, o_ref, kaug_ref, vaug_ref, s_ref, mb_ref, *, blk):
    p = pl.program_id(1)
    S = k_ref.shape[0]
    n_blk = S // blk
    half = blk // 2
    tri = lambda qi, c: qi * (qi + 1) // 2 + c

    for r in range(n_blk):
        rows = slice(r * blk, (r + 1) * blk)
        kf = k_ref[rows, :].astype(F32)
        vf = v_ref[rows, :].astype(F32)
        feat = kf_ref[rows, :]
        for j in range(HEAD_PAIR):
            kaug_ref[j, rows, :] = _with_decay_lanes(kf, feat, j, p)
            vaug_ref[j, rows, :] = jnp.where(_head_lane_mask(j, vf.shape), vf, 1.0).astype(BF16)

    for qi in range(n_blk):
        rows = slice(qi * blk, (qi + 1) * blk)
        qf = q_ref[rows, :].astype(F32) * (FOX_HEAD_DIM ** -0.5 * LOG2E)
        feat = qf_ref[rows, :]
        for j in range(HEAD_PAIR):
            qaug = _with_decay_lanes(qf, feat, j, p)
            ml = [None, None]

            def fold(ml, s, h):
                for g in range(s.shape[1] // LANES):
                    sg = s[:, g * LANES:(g + 1) * LANES]
                    ml[h] = sg if ml[h] is None else jnp.maximum(ml[h], sg)

            for c in range(qi):
                s = _dot_nt(qaug, kaug_ref[j, c * blk:(c + 1) * blk, :])
                s_ref[tri(qi, c), j] = s
                fold(ml, s[:half], 0)
                fold(ml, s[half:], 1)
            k_diag = kaug_ref[j, qi * blk:(qi + 1) * blk, :]
            row = lax.broadcasted_iota(jnp.int32, (half, half), 0)
            colk = lax.broadcasted_iota(jnp.int32, (half, half), 1)
            s_top = jnp.where(row >= colk, _dot_nt(qaug[:half], k_diag[:half]), NEG_INF)
            s_bot = _dot_nt(qaug[half:], k_diag)
            s_bot = jnp.concatenate([s_bot[:, :half], jnp.where(row >= colk, s_bot[:, half:], NEG_INF)],
                                    axis=1)
            s_ref[tri(qi, qi), j, :half, :half] = s_top
            s_ref[tri(qi, qi), j, half:, :] = s_bot
            fold(ml, s_top, 0)
            fold(ml, s_bot, 1)
            mb_ref[qi, j] = jnp.concatenate(
                [jnp.broadcast_to(jnp.max(m, axis=1, keepdims=True), (half, LANES)) for m in ml], axis=0)

    for qi in range(n_blk):
        outs = []
        for j in range(HEAD_PAIR):
            mb = jnp.concatenate([mb_ref[qi, j]] * (blk // LANES), axis=1)
            acc = None
            for c in range(qi):
                pr = jnp.exp2((s_ref[tri(qi, c), j] - mb).astype(BF16))
                d = _dot(pr, vaug_ref[j, c * blk:(c + 1) * blk, :])
                acc = d if acc is None else acc + d
            diag = tri(qi, qi)
            pr_top = jnp.exp2((s_ref[diag, j, :half, :half] - mb[:half, :half]).astype(BF16))
            pr_bot = jnp.exp2((s_ref[diag, j, half:, :] - mb[half:, :]).astype(BF16))
            d = jnp.concatenate([_dot(pr_top, vaug_ref[j, qi * blk:qi * blk + half, :]),
                                 _dot(pr_bot, vaug_ref[j, qi * blk:(qi + 1) * blk, :])], axis=0)
            acc = d if acc is None else acc + d
            outs.append(acc / pltpu.roll(acc, FOX_HEAD_DIM, 1))
        o_ref[qi * blk:(qi + 1) * blk, :] = jnp.where(
            _head_lane_mask(0, outs[0].shape), outs[0], outs[1]).astype(o_ref.dtype)


def _fox_attention(z, qfeat, kfeat, base):
    B, S, _ = z.shape
    blk = min(FOX_BLOCK, S)
    n_q = S // blk
    n_pair = FOX_WIDTH // LANES
    return pl.pallas_call(
        functools.partial(_fox_kernel, blk=blk),
        out_shape=jax.ShapeDtypeStruct((B, S, FOX_WIDTH), BF16),
        grid=(B, n_pair),
        in_specs=[pl.BlockSpec((None, S, LANES), lambda b, p: (b, 0, base + p)),
                  pl.BlockSpec((None, S, LANES), lambda b, p: (b, 0, base + n_pair + p)),
                  pl.BlockSpec((None, S, LANES), lambda b, p: (b, 0, base + 2 * n_pair + p)),
                  pl.BlockSpec((None, S, LANES), lambda b, p: (b, 0, 0)),
                  pl.BlockSpec((None, S, LANES), lambda b, p: (b, 0, 0))],
        out_specs=pl.BlockSpec((None, S, LANES), lambda b, p: (b, 0, p)),
        scratch_shapes=[pltpu.VMEM((HEAD_PAIR, S, LANES), BF16),
                        pltpu.VMEM((HEAD_PAIR, S, LANES), BF16),
                        pltpu.VMEM((n_q * (n_q + 1) // 2, HEAD_PAIR, blk, blk), F32),
                        pltpu.VMEM((n_q, HEAD_PAIR, blk, LANES), F32)],
        compiler_params=_params(2),
        name="fox_attention",
    )(z, z, z, qfeat, kfeat)


def _ca_kernel(q_ref, k_ref, v_ref, vec_ref, o_ref, u_ref, vaug_ref, s_ref, mb_ref, *, blk):
    S = q_ref.shape[0]
    left = CA_WINDOW_BLOCKS - 1
    n_blk = S // blk
    kblocks = [[(j - left + i, i) for i in range(CA_WINDOW_BLOCKS) if j - left + i >= 0]
               for j in range(n_blk)]

    @pl.when(pl.program_id(1) == 0)
    def _build_table():
        row_chunk = lax.broadcasted_iota(jnp.int32, (blk, blk), 0) // CHUNK
        col_chunk = lax.broadcasted_iota(jnp.int32, (blk, blk), 1) // CHUNK
        for hh in range(HEAD_PAIR):
            vec = jnp.broadcast_to(vec_ref[hh:hh + 1, :], (blk, vec_ref.shape[1]))
            toep = pltpu.roll(vec, 0, 1, stride=1, stride_axis=0)
            for i in range(CA_WINDOW_BLOCKS):
                dchunk = row_chunk - col_chunk + (left - i) * (blk // CHUNK)
                valid = (dchunk >= 0) & (dchunk <= CA_LEFT_CHUNKS)
                u_ref[hh, i] = jnp.where(valid, toep[:, i * blk:(i + 1) * blk], NEG_INF)

    for r in range(n_blk):
        rows = slice(r * blk, (r + 1) * blk)
        vf = v_ref[rows, :].astype(F32)
        for hh in range(HEAD_PAIR):
            vaug_ref[hh, rows, :] = jnp.where(_head_lane_mask(hh, vf.shape), vf, 1.0).astype(BF16)

    for j in range(n_blk):
        qf = q_ref[j * blk:(j + 1) * blk, :].astype(F32) * (CA_HEAD_DIM ** -0.5 * LOG2E)
        for hh in range(HEAD_PAIR):
            qm = jnp.where(_head_lane_mask(hh, qf.shape), qf, 0.0).astype(BF16)
            ml = None
            for kb, i in kblocks[j]:
                s = _dot_nt(qm, k_ref[kb * blk:(kb + 1) * blk, :]) + u_ref[hh, i]
                s_ref[j, hh, i] = s
                for g in range(blk // LANES):
                    sg = s[:, g * LANES:(g + 1) * LANES]
                    ml = sg if ml is None else jnp.maximum(ml, sg)
            mb_ref[j, hh] = jnp.broadcast_to(jnp.max(ml, axis=1, keepdims=True), (blk, LANES))

    for j in range(n_blk):
        outs = []
        for hh in range(HEAD_PAIR):
            mb = jnp.concatenate([mb_ref[j, hh]] * (blk // LANES), axis=1)
            acc = None
            for kb, i in kblocks[j]:
                pr = jnp.exp2((s_ref[j, hh, i] - mb).astype(BF16))
                d = _dot(pr, vaug_ref[hh, kb * blk:(kb + 1) * blk, :])
                acc = d if acc is None else acc + d
            outs.append(acc / pltpu.roll(acc, CA_HEAD_DIM, 1))
        o_ref[j * blk:(j + 1) * blk, :] = jnp.where(
            _head_lane_mask(0, outs[0].shape), outs[0], outs[1]).astype(o_ref.dtype)


def _ca_bias_vector(rel_bias, blk):
    n_m = CA_WINDOW_BLOCKS * blk
    shift = (CA_WINDOW_BLOCKS - 1) * blk
    period = n_m + blk
    x = np.arange(period)
    x = np.where(x >= n_m, x - period, x)
    vec = rel_bias.astype(F32)[:, np.clip(shift - x, REL_MIN, REL_MAX) - REL_MIN] * LOG2E
    return vec.reshape(rel_bias.shape[0] // HEAD_PAIR, HEAD_PAIR, period)


def _chunk_attention(z, bias_vec, base):
    B, S, _ = z.shape
    blk = ATT_BLOCK
    n_pair = CA_WIDTH // LANES
    period = bias_vec.shape[2]
    return pl.pallas_call(
        functools.partial(_ca_kernel, blk=blk),
        out_shape=jax.ShapeDtypeStruct((B, S, CA_WIDTH), BF16),
        grid=(n_pair, B),
        in_specs=[pl.BlockSpec((None, S, LANES), lambda p, b: (b, 0, base + p)),
                  pl.BlockSpec((None, S, LANES), lambda p, b: (b, 0, base + n_pair + p)),
                  pl.BlockSpec((None, S, LANES), lambda p, b: (b, 0, base + 2 * n_pair + p)),
                  pl.BlockSpec((None, HEAD_PAIR, period), lambda p, b: (p, 0, 0))],
        out_specs=pl.BlockSpec((None, S, LANES), lambda p, b: (b, 0, p)),
        scratch_shapes=[pltpu.VMEM((HEAD_PAIR, CA_WINDOW_BLOCKS, blk, blk), F32),
                        pltpu.VMEM((HEAD_PAIR, S, LANES), BF16),
                        pltpu.VMEM((S // blk, HEAD_PAIR, CA_WINDOW_BLOCKS, blk, blk), F32),
                        pltpu.VMEM((S // blk, HEAD_PAIR, blk, LANES), F32)],
        compiler_params=_params(2),
        name="chunk_attention",
    )(z, z, z, bias_vec)


def _s5_kernel(x_ref, wsu_ref, bsu_ref, bbr_ref, bbi_ref, cc_ref, ar_ref, ai_ref, d_ref, wg_ref, o_ref,
               ut_ref, xr_ref, xi_ref, yt_ref, hr_ref, hi_ref, *, steps, batch):
    @pl.when(pl.program_id(0) == 0)
    def _init():
        hr_ref[...] = jnp.zeros_like(hr_ref)
        hi_ref[...] = jnp.zeros_like(hi_ref)

    su = _dot(x_ref[...].reshape(batch * steps, x_ref.shape[2]), wsu_ref[...]) + bsu_ref[...]
    for b in range(batch):
        for i in range(SSM_SLABS):
            ut_ref[i, pl.ds(b, steps, stride=batch), :] = su[b * steps:(b + 1) * steps,
                                                             i * LANES:(i + 1) * LANES]

    slab = lambda i: slice(i * SSM_SLAB_STATE, (i + 1) * SSM_SLAB_STATE)
    for i in range(SSM_SLABS):
        ui = ut_ref[i].astype(BF16)
        xr_ref[:, slab(i)] = _dot(ui, bbr_ref[i])
        xi_ref[:, slab(i)] = _dot(ui, bbi_ref[i])
    for i in range(SSM_SLABS):
        cols = slab(i)
        ar = ar_ref[:, cols]
        ai = ai_ref[:, cols]
        hr = hr_ref[:, cols]
        hi = hi_ref[:, cols]
        for t in range(steps):
            rows = slice(t * batch, (t + 1) * batch)
            hr, hi = (ar * hr - ai * hi + xr_ref[rows, cols], ar * hi + ai * hr + xi_ref[rows, cols])
            xr_ref[rows, cols] = hr
            xi_ref[rows, cols] = hi
        hr_ref[:, cols] = hr
        hi_ref[:, cols] = hi
    ys = []
    for i in range(SSM_SLABS):
        cols = slab(i)
        h = jnp.concatenate([xr_ref[:, cols].astype(BF16), xi_ref[:, cols].astype(BF16)], axis=1)
        ys.append(_dot(h, cc_ref[i]) + d_ref[:, i * LANES:(i + 1) * LANES] * ut_ref[i])
    g = jax.nn.gelu(jnp.concatenate(ys, axis=1)).astype(BF16)
    half = wg_ref.shape[1] // 2
    for c in range(half // MERGE_SUB_COLS):
        cols = slice(c * MERGE_SUB_COLS, (c + 1) * MERGE_SUB_COLS)
        gcols = slice(half + c * MERGE_SUB_COLS, half + (c + 1) * MERGE_SUB_COLS)
        out = _dot(g, wg_ref[:, cols]) * jax.nn.sigmoid(_dot(g, wg_ref[:, gcols]))
        for k in range(MERGE_SUB_COLS // LANES):
            yt_ref[c * (MERGE_SUB_COLS // LANES) + k] = out[:, k * LANES:(k + 1) * LANES]
    for b in range(batch):
        for k in range(half // LANES):
            o_ref[b, :, k * LANES:(k + 1) * LANES] = (
                yt_ref[k, pl.ds(b, steps, stride=batch), :].astype(o_ref.dtype))


def _s5_branch(xb, l, w_su, b_su, bbr, bbi, cc, ar, ai, d, w_glu):
    B, S, D = xb.shape
    steps = min(SSM_TIME_BLOCK, S)
    tr = steps * B
    n_state = SSM_GROUPS * SSM_STATE
    d_out = w_glu.shape[2] // 2
    return pl.pallas_call(
        functools.partial(_s5_kernel, steps=steps, batch=B),
        out_shape=jax.ShapeDtypeStruct((B, S, d_out), BF16),
        grid=(S // steps,),
        in_specs=[pl.BlockSpec((B, steps, D), lambda t: (0, t, 0)),
                  _layer(w_su, l), _whole(b_su), _whole(bbr), _whole(bbi), _whole(cc), _whole(ar),
                  _whole(ai), _whole(d), _layer(w_glu, l)],
        out_specs=pl.BlockSpec((B, steps, d_out), lambda t: (0, t, 0)),
        scratch_shapes=[pltpu.VMEM((SSM_SLABS, tr, LANES), F32),
                        pltpu.VMEM((tr, n_state), F32), pltpu.VMEM((tr, n_state), F32),
                        pltpu.VMEM((d_out // LANES, tr, LANES), F32),
                        pltpu.VMEM((B, n_state), F32), pltpu.VMEM((B, n_state), F32)],
        compiler_params=_params(1),
        name="s5_branch",
    )(xb, w_su, b_su, bbr, bbi, cc, ar, ai, d, w_glu)


def _s5_params(lam_re, lam_im, log_dt, b_re, b_im, c_re, c_im, batch):
    lr = jnp.minimum(lam_re.astype(F32), -1e-4)
    li = lam_im.astype(F32)
    dt = jnp.exp(log_dt.astype(F32))[:, None]
    mag = jnp.exp(lr * dt)
    ar = mag * jnp.cos(li * dt)
    ai = mag * jnp.sin(li * dt)
    den = lr * lr + li * li
    gr = ((ar - 1.0) * lr + ai * li) / den
    gi = (ai * lr - (ar - 1.0) * li) / den
    br, bi = b_re.astype(F32), b_im.astype(F32)
    bbr = gr[..., None] * br - gi[..., None] * bi
    bbi = gr[..., None] * bi + gi[..., None] * br
    eye = jnp.eye(SSM_SLAB_GROUPS, dtype=F32)

    def in_map(bb):
        bb = bb.reshape(SSM_SLABS, SSM_SLAB_GROUPS, SSM_STATE, SSM_GROUP)
        return jnp.einsum('sgpc,gh->sgchp', bb, eye).reshape(SSM_SLABS, LANES, SSM_SLAB_STATE).astype(BF16)

    def out_map(c):
        c = c.astype(F32).reshape(SSM_SLABS, SSM_SLAB_GROUPS, SSM_GROUP, SSM_STATE)
        return jnp.einsum('sgcp,gh->sgphc', c, eye).reshape(SSM_SLABS, SSM_SLAB_STATE, LANES)

    cc = jnp.concatenate([out_map(c_re), -out_map(c_im)], axis=1).astype(BF16)
    n_state = SSM_GROUPS * SSM_STATE
    ar_b = jnp.broadcast_to(ar.reshape(1, n_state), (batch, n_state))
    ai_b = jnp.broadcast_to(ai.reshape(1, n_state), (batch, n_state))
    return in_map(bbr), in_map(bbi), cc, ar_b, ai_b


def _merge_kernel(x_ref, xb_ref, ya_ref, yb_ref, yc_ref, wg_ref, bg_ref, wa_ref, wc_ref, wo_ref,
                  lg_ref, lb_ref, of_ref, ob_ref, m_ref, *, alpha):
    d = x_ref.shape[1]
    xb = xb_ref[...]
    ya = ya_ref[...]
    yc = yc_ref[...]
    for c in range(d // MERGE_SUB_COLS):
        cols = slice(c * MERGE_SUB_COLS, (c + 1) * MERGE_SUB_COLS)

        def gate(k, cols=cols):
            gcols = slice(k * d + cols.start, k * d + cols.stop)
            return jax.nn.sigmoid(_dot(xb, wg_ref[:, gcols]) + bg_ref[:, gcols])

        merged = (gate(0) * _dot(ya, wa_ref[:, cols])
                  + gate(1) * yb_ref[:, cols].astype(F32)
                  + gate(2) * _dot(yc, wc_ref[:, cols]))
        m_ref[:, cols] = merged.astype(BF16)
    h = _dot(m_ref[...], wo_ref[...])
    y = _layer_norm(alpha * x_ref[...] + h, lg_ref[...], lb_ref[...])
    of_ref[...] = y
    ob_ref[...] = y.astype(BF16)


def _merge(x, xb, ya, yb, yc, l, w_gate, b_gate, w_fox_o, w_ca_o, w_o, lg, lb, alpha, tm):
    B, S, D = x.shape
    tm = min(tm, S)
    row = lambda w: pl.BlockSpec((None, tm, w), lambda b, i: (b, i, 0))
    return pl.pallas_call(
        functools.partial(_merge_kernel, alpha=alpha),
        out_shape=(jax.ShapeDtypeStruct((B, S, D), F32), jax.ShapeDtypeStruct((B, S, D), BF16)),
        grid=(B, S // tm),
        in_specs=[row(D), row(D), row(ya.shape[2]), row(D), row(yc.shape[2]),
                  _layer(w_gate, l), _whole(b_gate), _layer(w_fox_o, l), _layer(w_ca_o, l),
                  _layer(w_o, l), _whole(lg), _whole(lb)],
        out_specs=(row(D), row(D)),
        scratch_shapes=[pltpu.VMEM((tm, D), BF16)],
        compiler_params=_params(2),
        name="merge_proj_ln",
    )(x, xb, ya, yb, yc, w_gate, b_gate, w_fox_o, w_ca_o, w_o, lg, lb)


def _xattn_kernel(x_ref, xb_ref, kv_ref, wq_ref, wo_ref, lg_ref, lb_ref, of_ref, ob_ref, *, alpha):
    d = x_ref.shape[1]
    dh = d // XA_HEADS
    xb = xb_ref[...]
    heads = range(XA_HEADS)
    head = lambda h: slice(h * dh, (h + 1) * dh)
    q = [(_dot(xb, wq_ref[:, head(h)]) * (dh ** -0.5 * LOG2E)).astype(BF16) for h in heads]
    s = [_dot_nt(q[h], kv_ref[:, head(h)]) for h in heads]
    m = [jnp.max(s[h], axis=-1, keepdims=True) for h in heads]
    pr = [jnp.exp2(s[h] - m[h]) for h in heads]
    l = [jnp.sum(pr[h], axis=-1, keepdims=True) for h in heads]
    outs = [(_dot(pr[h].astype(BF16), kv_ref[:, d + h * dh:d + (h + 1) * dh]) / l[h]).astype(BF16)
            for h in heads]
    hout = _dot(jnp.concatenate(outs, axis=1), wo_ref[...])
    y = _layer_norm(alpha * x_ref[...] + hout, lg_ref[...], lb_ref[...])
    of_ref[...] = y
    ob_ref[...] = y.astype(BF16)


def _cross_attention(x, xb, kv, l, wq, wo, lg, lb, alpha, tm):
    B, S, D = x.shape
    M = kv.shape[2]
    tm = min(tm, S)
    row = pl.BlockSpec((None, tm, D), lambda b, i: (b, i, 0))
    return pl.pallas_call(
        functools.partial(_xattn_kernel, alpha=alpha),
        out_shape=(jax.ShapeDtypeStruct((B, S, D), F32), jax.ShapeDtypeStruct((B, S, D), BF16)),
        grid=(B, S // tm),
        in_specs=[row, row, pl.BlockSpec((None, None, M, 2 * D), lambda b, i: (l, b, 0, 0)),
                  _layer(wq, l), _layer(wo, l), _whole(lg), _whole(lb)],
        out_specs=(row, row),
        compiler_params=_params(2),
        name="cross_attention_ln",
    )(x, xb, kv, wq, wo, lg, lb)


def _ffn_kernel(x_ref, xb_ref, wup_ref, cw_ref, cb_ref, wdn_ref, lg_ref, lb_ref, of_ref, ob_ref,
                abuf_ref, gbuf_ref, act_ref, *, alpha, d_ff):
    tm = x_ref.shape[0]
    pad = FFN_HALO_ROWS

    @pl.when(pl.program_id(1) == 0)
    def _seq_start():
        abuf_ref[0:pad, :] = jnp.zeros((pad, d_ff), F32)

    xb = xb_ref[...]
    h = jnp.zeros(x_ref.shape, F32)
    n_sub = d_ff // FFN_SUB_COLS
    for j in range(n_sub):
        cols = slice(j * FFN_SUB_COLS, (j + 1) * FFN_SUB_COLS)
        gcols = slice(d_ff + j * FFN_SUB_COLS, d_ff + (j + 1) * FFN_SUB_COLS)
        abuf_ref[pad:pad + tm, cols] = _dot(xb, wup_ref[:, cols])
        gbuf_ref[:, cols] = _dot(xb, wup_ref[:, gcols])
    for j in range(n_sub):
        cols = slice(j * FFN_SUB_COLS, (j + 1) * FFN_SUB_COLS)
        a = abuf_ref[pad:pad + tm, cols]
        a1 = abuf_ref[pad - 1:pad - 1 + tm, cols]
        a2 = abuf_ref[pad - 2:pad - 2 + tm, cols]
        conv = (cw_ref[0:1, cols] * a2 + cw_ref[1:2, cols] * a1 + cw_ref[2:3, cols] * a
                + cb_ref[:, cols])
        act_ref[:, cols] = (jax.nn.gelu(conv) * gbuf_ref[:, cols]).astype(BF16)
        if (j + 1) % FFN_DOWN_GROUP == 0 or j == n_sub - 1:
            lo = (j // FFN_DOWN_GROUP) * FFN_DOWN_GROUP * FFN_SUB_COLS
            rows = slice(lo, (j + 1) * FFN_SUB_COLS)
            h = h + _dot(act_ref[:, rows], wdn_ref[rows, :])
    abuf_ref[0:pad, :] = abuf_ref[tm:tm + pad, :]
    y = _layer_norm(alpha * x_ref[...] + h, lg_ref[...], lb_ref[...])
    of_ref[...] = y
    ob_ref[...] = y.astype(BF16)


def _conv_ffn(x, xb, l, w_up, conv_w, conv_b, w_down, lg, lb, alpha, tm):
    B, S, D = x.shape
    d_ff = w_down.shape[1]
    tm = min(tm, S)
    row = pl.BlockSpec((None, tm, D), lambda b, i: (b, i, 0))
    once = pl.Buffered(1)
    return pl.pallas_call(
        functools.partial(_ffn_kernel, alpha=alpha, d_ff=d_ff),
        out_shape=(jax.ShapeDtypeStruct((B, S, D), F32), jax.ShapeDtypeStruct((B, S, D), BF16)),
        grid=(B, S // tm),
        in_specs=[row, row, _layer(w_up, l, pipeline_mode=once), _whole(conv_w), _whole(conv_b),
                  _layer(w_down, l, pipeline_mode=once), _whole(lg), _whole(lb)],
        out_specs=(row, row),
        scratch_shapes=[pltpu.VMEM((tm + FFN_HALO_ROWS, d_ff), F32), pltpu.VMEM((tm, d_ff), F32),
                        pltpu.VMEM((tm, d_ff), BF16)],
        compiler_params=_params(2),
        name="conv_ffn_ln",
    )(x, xb, w_up, conv_w, conv_b, w_down, lg, lb)


def _split_in_proj(w_in, b_in, d_model):
    o = np.cumsum([0, FOX_WIDTH, FOX_WIDTH, FOX_WIDTH, FOX_HEADS, SSM_WIDTH,
                   CA_WIDTH, CA_WIDTH, CA_WIDTH, 3 * d_model])
    fqkv, ff, su, cqkv, gates = (slice(o[0], o[3]), slice(o[3], o[4]), slice(o[4], o[5]),
                                 slice(o[5], o[8]), slice(o[8], o[9]))
    cols = lambda s: (w_in[..., s].astype(BF16), b_in[..., s].astype(F32)[:, None, :])
    cat = lambda a, b: tuple(jnp.concatenate(p, axis=-1) for p in zip(a, b))
    pad = ((0, 0), (0, 0), (0, LANES - FOX_HEADS))
    w_ff, b_ff = cols(ff)
    return cat(cols(fqkv), cols(cqkv)), cols(gates), cols(su), (jnp.pad(w_ff, pad), jnp.pad(b_ff, pad))


def kernel(x, mem, w_in, b_in, ssm_lambda_re, ssm_lambda_im, ssm_log_dt, ssm_b_re, ssm_b_im, ssm_c_re, ssm_c_im, ssm_d, ca_rel_bias, w_fox_o, w_ssm_glu, w_ca_o, w_o, xa_wq, xa_wkv, xa_wo, ffn_w_up, ffn_conv_w, ffn_conv_b, ffn_w_down, ln_g, ln_b):
    B, S, D = x.shape
    depth = w_in.shape[0]
    assert B == SUBLANES, "the S5 scan keeps one batch row per f32 sublane"
    assert S % FOX_BLOCK == 0 and S % ATT_BLOCK == 0 and S % SSM_TIME_BLOCK == 0
    assert ffn_conv_w.shape[1] == 3, "the ConvFFN kernel hard-codes the three causal taps"
    alpha = float((2 * depth) ** 0.25)
    mem_b = mem.astype(BF16)
    xf = x.astype(F32)
    xb = x.astype(BF16)
    zero_bias = jnp.zeros((1, 2 * D), F32)
    (w_qkv, b_qkv), (w_gate, b_gate), (w_su, b_su), (w_ff, b_ff) = _split_in_proj(w_in, b_in, D)
    w_fox_o, w_ssm_glu, w_ca_o, w_o, xa_wq, xa_wkv, xa_wo, ffn_w_up, ffn_w_down = (
        w.astype(BF16) for w in (w_fox_o, w_ssm_glu, w_ca_o, w_o, xa_wq, xa_wkv, xa_wo,
                                 ffn_w_up, ffn_w_down))
    kv = _memory_kv(mem_b, xa_wkv, zero_bias)
    for l in range(depth):
        z = _matmul(xb, w_qkv, l, b_qkv[l], BF16, tm=1024, name="in_proj")
        qfeat, kfeat = _forget_decay_features(xb, w_ff, l, b_ff[l])
        ya = _fox_attention(z, qfeat, kfeat, 0)
        yc = _chunk_attention(z, _ca_bias_vector(ca_rel_bias[l], ATT_BLOCK), 3 * FOX_WIDTH // LANES)
        bbr, bbi, cc, ar, ai = _s5_params(ssm_lambda_re[l], ssm_lambda_im[l], ssm_log_dt[l],
                                          ssm_b_re[l], ssm_b_im[l], ssm_c_re[l], ssm_c_im[l], B)
        yb = _s5_branch(xb, l, w_su, b_su[l], bbr, bbi, cc, ar, ai,
                        ssm_d[l].astype(F32)[None, :], w_ssm_glu)
        ln = lambda k: (ln_g[l, k].astype(F32)[None, :], ln_b[l, k].astype(F32)[None, :])
        xf, xb = _merge(xf, xb, ya, yb, yc, l, w_gate, b_gate[l], w_fox_o, w_ca_o, w_o,
                        *ln(0), alpha, tm=1024)
        xf, xb = _cross_attention(xf, xb, kv, l, xa_wq, xa_wo, *ln(1), alpha, tm=1024)
        xf, xb = _conv_ffn(xf, xb, l, ffn_w_up, ffn_conv_w[l].astype(F32),
                           ffn_conv_b[l].astype(F32)[None, :], ffn_w_down, *ln(2), alpha, tm=512)
    return xf.astype(x.dtype)
```
